```python
import math
import jax, jax.numpy as jnp
from jax import lax
import numpy as np

D_MODEL = 2048
BATCH = 2
SEQ = 8192
DEPTH = 1
DEC_BATCH = 8
DEC_SEQ = 4096
PAST_LEN = 128

N_MEM = 256
MLA_HEADS = 12
Q_LORA = 512
KV_LORA = 512
QK_NOPE = 128
QK_ROPE = 64
V_HEAD = 128
ROPE_THETA = 10000.0
DIL_PAIRS = ((128, 1), (512, 4), (2048, 16))
DIL_GROUPS = 3
DIL_HEADS_PER_GROUP = 4
DIL_HEADS = DIL_GROUPS * DIL_HEADS_PER_GROUP
DIL_HEAD_DIM = 128
X_HEADS = 4
X_HEAD_DIM = 256
NUM_BUCKETS = 32
MAX_DISTANCE = 1024
D_FF = 4 * D_MODEL
N_BRANCH = 3
Q_BLOCK = 128
EPS = 1e-6
NEG_INF = -1e30

SPLIT_SIZES = (Q_LORA, KV_LORA, QK_ROPE, 3 * DIL_HEADS * DIL_HEAD_DIM, X_HEADS * X_HEAD_DIM, N_BRANCH * D_MODEL)
D_IN = Q_LORA + KV_LORA + QK_ROPE + 3 * DIL_HEADS * DIL_HEAD_DIM + X_HEADS * X_HEAD_DIM + N_BRANCH * D_MODEL

kernel_name = "hybrid_mla_dilated_memory_encoder"


def rms_norm(x, g):
    x32 = x.astype(jnp.float32)
    y = x32 * lax.rsqrt(jnp.mean(x32 * x32, axis=-1, keepdims=True) + EPS)
    return (y * g.astype(jnp.float32)).astype(x.dtype)


def apply_rope(x, pos):
    half = QK_ROPE // 2
    inv = 1.0 / (ROPE_THETA ** (jnp.arange(half, dtype=jnp.float32) / half))
    ang = pos.astype(jnp.float32)[:, None] * inv[None, :]
    cos = jnp.cos(ang)[:, None, :]
    sin = jnp.sin(ang)[:, None, :]
    x32 = x.astype(jnp.float32)
    x1, x2 = x32[..., :half], x32[..., half:]
    return jnp.concatenate([x1 * cos - x2 * sin, x2 * cos + x1 * sin], axis=-1).astype(x.dtype)


def t5_bucket(rel):
    nb = NUM_BUCKETS // 2
    ret = (rel > 0).astype(np.int32) * nb
    n = np.abs(rel)
    max_exact = nb // 2
    large = max_exact + (np.log(np.maximum(n, 1) / max_exact) / np.log(MAX_DISTANCE / max_exact)
                         * (nb - max_exact)).astype(np.int32)
    large = np.minimum(large, nb - 1)
    return (ret + np.where(n < max_exact, n, large)).astype(np.int32)


def mla_attention(c_q, c_kv, k_rope, pos, g_qn, w_uq, g_kvn, w_ukv):
    B, S, _ = c_q.shape
    q = (rms_norm(c_q, g_qn) @ w_uq).reshape(B, S, MLA_HEADS, QK_NOPE + QK_ROPE)
    q = jnp.concatenate([q[..., :QK_NOPE], apply_rope(q[..., QK_NOPE:], pos)], axis=-1)
    kv = (rms_norm(c_kv, g_kvn) @ w_ukv).reshape(B, S, MLA_HEADS, QK_NOPE + V_HEAD)
    k_nope, v = kv[..., :QK_NOPE], kv[..., QK_NOPE:]
    k_pe = jnp.broadcast_to(apply_rope(k_rope[:, :, None, :], pos), (B, S, MLA_HEADS, QK_ROPE))
    k = jnp.concatenate([k_nope, k_pe], axis=-1)
    scale = (QK_NOPE + QK_ROPE) ** -0.5

    def block(q0):
        qb = lax.dynamic_slice_in_dim(q, q0, Q_BLOCK, axis=1)
        logits = jnp.einsum('bqhd,bkhd->bhqk', qb, k).astype(jnp.float32) * scale
        p = jax.nn.softmax(logits, axis=-1)
        return jnp.einsum('bhqk,bkhd->bqhd', p.astype(v.dtype), v)

    out = lax.map(block, jnp.arange(S // Q_BLOCK) * Q_BLOCK)
    return out.transpose(1, 0, 2, 3, 4).reshape(B, S, MLA_HEADS * V_HEAD)


def dilated_attention(q, k, v, rel_bias):
    B, S = q.shape[0], q.shape[1]
    scale = DIL_HEAD_DIM ** -0.5
    groups = []
    for g, (w, r) in enumerate(DIL_PAIRS):
        n_side = (w // 2) // r
        offs = np.arange(-n_side, n_side + 1, dtype=np.int32) * r
        pad = n_side * r
        kp = jnp.pad(k[:, :, g], ((0, 0), (pad, pad), (0, 0), (0, 0)))
        vp = jnp.pad(v[:, :, g], ((0, 0), (pad, pad), (0, 0), (0, 0)))
        bias = rel_bias[t5_bucket(offs)][:, g * DIL_HEADS_PER_GROUP:(g + 1) * DIL_HEADS_PER_GROUP].T
        groups.append((offs, pad, q[:, :, g], kp, vp, bias.astype(jnp.float32)))

    def block(q0):
        i = q0 + jnp.arange(Q_BLOCK)
        outs, lses = [], []
        for offs, pad, qg, kp, vp, bias in groups:
            kpos = i[:, None] + offs[None, :]
            kb = jnp.take(kp, kpos + pad, axis=1)
            vb = jnp.take(vp, kpos + pad, axis=1)
            qb = lax.dynamic_slice_in_dim(qg, q0, Q_BLOCK, axis=1)
            logits = jnp.einsum('bqhd,bqkhd->bhqk', qb, kb).astype(jnp.float32) * scale
            logits = logits + bias[None, :, None, :]
            valid = (kpos >= 0) & (kpos < S)
            logits = jnp.where(valid[None, None], logits, NEG_INF)
            m = jnp.max(logits, axis=-1, keepdims=True)
            p = jnp.exp(logits - m)
            s = jnp.sum(p, axis=-1, keepdims=True)
            outs.append(jnp.einsum('bhqk,bqkhd->bqhd', (p / s).astype(vb.dtype), vb))
            lses.append((m + jnp.log(s))[..., 0])
        alpha = jax.nn.softmax(jnp.stack(lses, axis=-1), axis=-1).transpose(0, 2, 1, 3)
        out = sum(alpha[..., gi, None].astype(outs[gi].dtype) * outs[gi] for gi in range(DIL_GROUPS))
        return out

    out = lax.map(block, jnp.arange(S // Q_BLOCK) * Q_BLOCK)
    return out.transpose(1, 0, 2, 3, 4).reshape(B, S, DIL_HEADS_PER_GROUP * DIL_HEAD_DIM)


def memory_cross_attention(q, mem_n, w_mkv):
    B, S = q.shape[0], q.shape[1]
    kv = (mem_n @ w_mkv).reshape(B, mem_n.shape[1], 2, X_HEADS, X_HEAD_DIM)
    k, v = kv[:, :, 0], kv[:, :, 1]
    logits = jnp.einsum('bshd,bmhd->bhsm', q, k).astype(jnp.float32) * (X_HEAD_DIM ** -0.5)
    p = jax.nn.softmax(logits, axis=-1)
    out = jnp.einsum('bhsm,bmhd->bshd', p.astype(v.dtype), v)
    return out.reshape(B, S, X_HEADS * X_HEAD_DIM)


def encoder_layer(x, mem, pos, rel_bias, g_attn, w_in, g_qn, w_uq, g_kvn, w_ukv, g_mem, w_mkv,
                  w_b_mla, w_b_dil, w_b_mem, w_out, g_mlp, w_up, w_down):
    B, S, _ = x.shape
    h = rms_norm(x, g_attn)
    proj = h @ w_in
    cuts = np.cumsum(SPLIT_SIZES)[:-1].tolist()
    c_q, c_kv, k_rope, dil_qkv, x_q, gate_logits = jnp.split(proj, cuts, axis=-1)

    o_mla = mla_attention(c_q, c_kv, k_rope, pos, g_qn, w_uq, g_kvn, w_ukv)
    dil = dil_qkv.reshape(B, S, 3, DIL_GROUPS, DIL_HEADS_PER_GROUP, DIL_HEAD_DIM)
    o_dil = dilated_attention(dil[:, :, 0], dil[:, :, 1], dil[:, :, 2], rel_bias)
    o_mem = memory_cross_attention(x_q.reshape(B, S, X_HEADS, X_HEAD_DIM), rms_norm(mem, g_mem), w_mkv)

    gates = jax.nn.sigmoid(gate_logits.astype(jnp.float32)).astype(x.dtype).reshape(B, S, N_BRANCH, D_MODEL)
    merged = (gates[:, :, 0] * (o_mla @ w_b_mla)
              + gates[:, :, 1] * (o_dil @ w_b_dil)
              + gates[:, :, 2] * (o_mem @ w_b_mem))
    x = x + merged @ w_out
    h = rms_norm(x, g_mlp)
    x = x + jnp.square(jax.nn.relu(h @ w_up)) @ w_down
    return x


def encoder_trunk(x, mem, rel_bias, g_attn, w_in, g_qn, w_uq, g_kvn, w_ukv, g_mem, w_mkv,
                  w_b_mla, w_b_dil, w_b_mem, w_out, g_mlp, w_up, w_down, g_final):
    pos = jnp.arange(x.shape[1])
    for l in range(DEPTH):
        x = encoder_layer(x, mem, pos, rel_bias, g_attn[l], w_in[l], g_qn[l], w_uq[l], g_kvn[l], w_ukv[l],
                          g_mem[l], w_mkv[l], w_b_mla[l], w_b_dil[l], w_b_mem[l], w_out[l],
                          g_mlp[l], w_up[l], w_down[l])
    return rms_norm(x, g_final)


def setup_inputs(seed: int = 0) -> dict:
    key = jax.random.key(seed)
    ks = jax.random.split(key, 32)
    f32 = jnp.float32

    def w(k, fan_in, shape):
        return jax.random.normal(k, shape, f32) * (fan_in ** -0.5)

    def gain(k, shape):
        return 1.0 + 0.02 * jax.random.normal(k, shape, f32)

    L = DEPTH
    return {
        "x_prompt": jax.random.normal(ks[0], (BATCH, SEQ, D_MODEL), f32),
        "x_sample": jax.random.normal(ks[1], (DEC_BATCH, DEC_SEQ, D_MODEL), f32),
        "mem_prompt": jax.random.normal(ks[2], (BATCH, N_MEM, D_MODEL), f32),
        "mem_sample": jax.random.normal(ks[3], (DEC_BATCH, N_MEM, D_MODEL), f32),
        "rel_bias": 0.5 * jax.random.normal(ks[4], (NUM_BUCKETS, DIL_HEADS), f32),
        "g_attn": gain(ks[5], (L, D_MODEL)),
        "w_in": w(ks[6], D_MODEL, (L, D_MODEL, D_IN)),
        "g_q_norm": gain(ks[7], (L, Q_LORA)),
        "w_uq": w(ks[8], Q_LORA, (L, Q_LORA, MLA_HEADS * (QK_NOPE + QK_ROPE))),
        "g_kv_norm": gain(ks[9], (L, KV_LORA)),
        "w_ukv": w(ks[10], KV_LORA, (L, KV_LORA, MLA_HEADS * (QK_NOPE + V_HEAD))),
        "g_mem": gain(ks[11], (L, D_MODEL)),
        "w_mem_kv": w(ks[12], D_MODEL, (L, D_MODEL, 2 * X_HEADS * X_HEAD_DIM)),
        "w_b_mla": w(ks[13], MLA_HEADS * V_HEAD, (L, MLA_HEADS * V_HEAD, D_MODEL)),
        "w_b_dil": w(ks[14], DIL_HEADS_PER_GROUP * DIL_HEAD_DIM, (L, DIL_HEADS_PER_GROUP * DIL_HEAD_DIM, D_MODEL)),
        "w_b_mem": w(ks[15], X_HEADS * X_HEAD_DIM, (L, X_HEADS * X_HEAD_DIM, D_MODEL)),
        "w_out": w(ks[16], D_MODEL, (L, D_MODEL, D_MODEL)),
        "g_mlp": gain(ks[17], (L, D_MODEL)),
        "w_up": w(ks[18], D_MODEL, (L, D_MODEL, D_FF)),
        "w_down": w(ks[19], D_FF, (L, D_FF, D_MODEL)),
        "g_final": gain(ks[20], (D_MODEL,)),
    }


def reference(x_prompt, x_sample, mem_prompt, mem_sample, rel_bias, g_attn, w_in, g_q_norm, w_uq,
              g_kv_norm, w_ukv, g_mem, w_mem_kv, w_b_mla, w_b_dil, w_b_mem, w_out, g_mlp, w_up, w_down,
              g_final):
    y_prompt = encoder_trunk(x_prompt, mem_prompt, rel_bias, g_attn, w_in, g_q_norm, w_uq, g_kv_norm, w_ukv,
                             g_mem, w_mem_kv, w_b_mla, w_b_dil, w_b_mem, w_out, g_mlp, w_up, w_down, g_final)
    y_sample = encoder_trunk(x_sample, mem_sample, rel_bias, g_attn, w_in, g_q_norm, w_uq, g_kv_norm, w_ukv,
                             g_mem, w_mem_kv, w_b_mla, w_b_dil, w_b_mem, w_out, g_mlp, w_up, w_down, g_final)
    return (y_prompt, y_sample)
```

```python
import functools
import math

import numpy as np
import jax
import jax.numpy as jnp
from jax import lax
from jax.experimental import pallas as pl
from jax.experimental.pallas import tpu as pltpu

F32 = jnp.float32
BF16 = jnp.bfloat16

D_MODEL = 2048
N_MEM = 256
MLA_HEADS = 12
Q_LORA = 512
KV_LORA = 512
QK_NOPE = 128
QK_ROPE = 64
V_HEAD = 128
ROPE_THETA = 10000.0
DIL_PAIRS = ((128, 1), (512, 4), (2048, 16))
DIL_GROUPS = 3
DIL_HEADS_PER_GROUP = 4
DIL_HEAD_DIM = 128
X_HEADS = 4
X_HEAD_DIM = 256
NUM_BUCKETS = 32
MAX_DISTANCE = 1024
D_FF = 4 * D_MODEL
N_BRANCH = 3
EPS = 1e-6
NEG_INF = -1e30

DIL_W = DIL_HEADS_PER_GROUP * DIL_HEAD_DIM
DIL_SIDE = 64
DIL_QB = 128
DIL_KB = DIL_QB + 2 * DIL_SIDE
QK_PAD = 256

_PROJ_GATE = 0
_PROJ_XQ = _PROJ_GATE + N_BRANCH * D_MODEL
_PROJ_DQ = _PROJ_XQ + X_HEADS * X_HEAD_DIM
_PROJ_DK = _PROJ_DQ + DIL_GROUPS * DIL_W
_PROJ_DV = _PROJ_DK + DIL_GROUPS * DIL_W
_PROJ_CQ = _PROJ_DV + DIL_GROUPS * DIL_W
_PROJ_CKV = _PROJ_CQ + Q_LORA
_PROJ_KR = _PROJ_CKV + KV_LORA
PROJ_W = _PROJ_KR + 512

VMEM_LIMIT = 52 * 1024 * 1024


def _cparams(n_axes):
    return pltpu.CompilerParams(dimension_semantics=("arbitrary",) * n_axes, vmem_limit_bytes=VMEM_LIMIT)


def _rms(x32, g32):
    return x32 * lax.rsqrt(jnp.mean(x32 * x32, axis=-1, keepdims=True) + EPS) * g32


def _dot(a, b):
    return jnp.dot(a, b, preferred_element_type=F32)


def _dot_nt(a, b):
    return lax.dot_general(a, b, (((1,), (1,)), ((), ())), preferred_element_type=F32)


def _in_proj_kernel(x_ref, g_ref, w_ref, o_ref, h_ref):
    @pl.when(pl.program_id(1) == 0)
    def _():
        h_ref[...] = _rms(x_ref[...], g_ref[...]).astype(BF16)

    o_ref[...] = _dot(h_ref[...], w_ref[...]).astype(BF16)


def _in_proj(x2, g, w, *, tm=1024, tn=1024):
    T, D = x2.shape
    return pl.pallas_call(
        _in_proj_kernel,
        grid=(T // tm, PROJ_W // tn),
        in_specs=[
            pl.BlockSpec((tm, D), lambda i, j: (i, 0)),
            pl.BlockSpec((1, D), lambda i, j: (0, 0)),
            pl.BlockSpec((D, tn), lambda i, j: (0, j)),
        ],
        out_specs=pl.BlockSpec((tm, tn), lambda i, j: (i, j)),
        out_shape=jax.ShapeDtypeStruct((T, PROJ_W), BF16),
        scratch_shapes=[pltpu.VMEM((tm, D), BF16)],
        compiler_params=_cparams(2),
        name="in_proj",
    )(x2, g, w)


def _mla_proj_kernel(cq_ref, ckv_ref, kr_ref, cos_ref, sin_ref, gq_ref, gkv_ref, wqa_ref, wqb_ref, wkv_ref,
                     q_ref, k_ref, v_ref, *, q_scale):
    cos = cos_ref[...]
    sin = sin_ref[...]
    nq = _rms(cq_ref[...].astype(F32), gq_ref[...]).astype(BF16)
    qa = _dot(nq, wqa_ref[...])
    qb = _dot(nq, wqb_ref[...])
    nkv = _rms(ckv_ref[...].astype(F32), gkv_ref[...]).astype(BF16)
    kv = _dot(nkv, wkv_ref[...])
    kr = kr_ref[...].astype(F32)
    kpe = (kr[:, 0:128] * cos + kr[:, 128:256] * sin).astype(BF16)
    for h in range(MLA_HEADS):
        c0 = h * QK_PAD
        q_ref[:, c0:c0 + 128] = (qa[:, c0:c0 + 128] * q_scale).astype(BF16)
        pe = qa[:, c0 + 128:c0 + 256] * cos + qb[:, h * 128:(h + 1) * 128] * sin
        q_ref[:, c0 + 128:c0 + 256] = (pe * q_scale).astype(BF16)
        k_ref[:, c0:c0 + 128] = kv[:, c0:c0 + 128].astype(BF16)
        k_ref[:, c0 + 128:c0 + 256] = kpe
        v_ref[:, h * 128:(h + 1) * 128] = kv[:, c0 + 128:c0 + 256].astype(BF16)


def _mla_proj(proj, cos_t, sin_t, gq, gkv, wqa, wqb, wkv, *, S, q_scale, tm=512):
    T = proj.shape[0]
    nsb = S // tm
    HQ = MLA_HEADS * QK_PAD
    HV = MLA_HEADS * V_HEAD
    const = lambda i: (0, 0)
    return pl.pallas_call(
        functools.partial(_mla_proj_kernel, q_scale=q_scale),
        grid=(T // tm,),
        in_specs=[
            pl.BlockSpec((tm, Q_LORA), lambda i: (i, _PROJ_CQ // 512)),
            pl.BlockSpec((tm, KV_LORA), lambda i: (i, _PROJ_CKV // 512)),
            pl.BlockSpec((tm, 512), lambda i: (i, _PROJ_KR // 512)),
            pl.BlockSpec((tm, 128), lambda i: (i % nsb, 0)),
            pl.BlockSpec((tm, 128), lambda i: (i % nsb, 0)),
            pl.BlockSpec((1, Q_LORA), const),
            pl.BlockSpec((1, KV_LORA), const),
            pl.BlockSpec(wqa.shape, const, pipeline_mode=pl.Buffered(1)),
            pl.BlockSpec(wqb.shape, const, pipeline_mode=pl.Buffered(1)),
            pl.BlockSpec(wkv.shape, const, pipeline_mode=pl.Buffered(1)),
        ],
        out_specs=[
            pl.BlockSpec((tm, HQ), lambda i: (i, 0)),
            pl.BlockSpec((tm, HQ), lambda i: (i, 0)),
            pl.BlockSpec((tm, HV), lambda i: (i, 0)),
        ],
        out_shape=[
            jax.ShapeDtypeStruct((T, HQ), BF16),
            jax.ShapeDtypeStruct((T, HQ), BF16),
            jax.ShapeDtypeStruct((T, HV), BF16),
        ],
        compiler_params=_cparams(1),
        name="mla_proj",
    )(proj, proj, proj, cos_t, sin_t, gq, gkv, wqa, wqb, wkv)


def _mla_flash_kernel(q_ref, k_ref, v_ref, o_ref, *, tk, nk):
    q = q_ref[...]
    tq = q.shape[0]

    def body(j, carry):
        m, l, acc = carry
        r0 = pl.multiple_of(j * tk, tk)
        s = _dot_nt(q, k_ref[pl.ds(r0, tk), :])
        m_new = jnp.maximum(m, jnp.max(s, axis=-1, keepdims=True))
        alpha = jnp.exp2(m - m_new)
        p = jnp.exp2(s - m_new)
        l = alpha * l + jnp.sum(p, axis=-1, keepdims=True)
        acc = alpha * acc + _dot(p.astype(BF16), v_ref[pl.ds(r0, tk), :])
        return m_new, l, acc

    init = (jnp.full((tq, 1), -jnp.inf, F32), jnp.zeros((tq, 1), F32), jnp.zeros((tq, V_HEAD), F32))
    _, l, acc = lax.fori_loop(0, nk, body, init)
    o_ref[...] = (acc / l).astype(BF16)


def _mla_flash(q, k, v, *, B, S, tq=512, tk=512):
    T = B * S
    nqb = S // tq
    return pl.pallas_call(
        functools.partial(_mla_flash_kernel, tk=tk, nk=S // tk),
        grid=(B, MLA_HEADS, nqb),
        in_specs=[
            pl.BlockSpec((tq, QK_PAD), lambda b, h, i: (b * nqb + i, h)),
            pl.BlockSpec((S, QK_PAD), lambda b, h, i: (b, h)),
            pl.BlockSpec((S, V_HEAD), lambda b, h, i: (b, h)),
        ],
        out_specs=pl.BlockSpec((tq, V_HEAD), lambda b, h, i: (b * nqb + i, h)),
        out_shape=jax.ShapeDtypeStruct((T, MLA_HEADS * V_HEAD), BF16),
        compiler_params=_cparams(3),
        name="mla_flash",
    )(q, k, v)


def _dil_kernel(q_ref, k_ref, v_ref, bias_ref, o_ref, lse_ref, *, L, ta):
    scale = DIL_HEAD_DIM ** -0.5
    lane = lax.broadcasted_iota(jnp.int32, (DIL_QB, 128), 1)
    base = pl.program_id(2) * ta
    for t in range(ta // DIL_QB):
        a0 = base + t * DIL_QB
        k0 = jnp.clip(a0 - DIL_SIDE, 0, L - DIL_KB)
        var = (a0 - k0) // DIL_SIDE
        k0 = pl.multiple_of(k0, DIL_SIDE)
        lses = []
        for h in range(DIL_HEADS_PER_GROUP):
            cs = slice(h * DIL_HEAD_DIM, (h + 1) * DIL_HEAD_DIM)
            qh = q_ref[0, t * DIL_QB:(t + 1) * DIL_QB, cs]
            kh = k_ref[0, pl.ds(k0, DIL_KB), cs]
            vh = v_ref[0, pl.ds(k0, DIL_KB), cs]
            s = _dot_nt(qh, kh) * scale + bias_ref[var, h]
            m = jnp.max(s, axis=-1, keepdims=True)
            p = jnp.exp(s - m)
            l = jnp.sum(p, axis=-1, keepdims=True)
            o = _dot(p.astype(BF16), vh) / l
            o_ref[0, t * DIL_QB:(t + 1) * DIL_QB, cs] = o
            lses.append(m + jnp.log(l))
        packed = jnp.where(lane < 32, lses[0], jnp.where(lane < 64, lses[1], jnp.where(lane < 96, lses[2], lses[3])))
        lse_ref[0, t * DIL_QB:(t + 1) * DIL_QB, :] = packed


def _dilated_group(proj3, bias, *, B, S, g, r):
    L = S // r
    ta = min(L, 512)
    wb = PROJ_W // DIL_W
    out, lse = pl.pallas_call(
        functools.partial(_dil_kernel, L=L, ta=ta),
        grid=(B, r, L // ta),
        in_specs=[
            pl.BlockSpec((1, ta, DIL_W), lambda b, c, a: (b, a, c * wb + _PROJ_DQ // DIL_W + g)),
            pl.BlockSpec((1, L, DIL_W), lambda b, c, a: (b, 0, c * wb + _PROJ_DK // DIL_W + g)),
            pl.BlockSpec((1, L, DIL_W), lambda b, c, a: (b, 0, c * wb + _PROJ_DV // DIL_W + g)),
            pl.BlockSpec(bias.shape, lambda b, c, a: (0, 0, 0, 0)),
        ],
        out_specs=[
            pl.BlockSpec((1, ta, DIL_W), lambda b, c, a: (b, a, c)),
            pl.BlockSpec((1, ta, 128), lambda b, c, a: (b, a, c)),
        ],
        out_shape=[
            jax.ShapeDtypeStruct((B, L, r * DIL_W), F32),
            jax.ShapeDtypeStruct((B, L, r * 128), F32),
        ],
        compiler_params=_cparams(3),
        name=f"dilated_g{g}",
    )(proj3, proj3, proj3, bias)
    return out.reshape(B * S, DIL_W), lse.reshape(B * S, 128)


def _t5_bucket(rel):
    nb = NUM_BUCKETS // 2
    ret = (rel > 0).astype(np.int32) * nb
    n = np.abs(rel)
    max_exact = nb // 2
    large = max_exact + (np.log(np.maximum(n, 1) / max_exact) / np.log(MAX_DISTANCE / max_exact)
                         * (nb - max_exact)).astype(np.int32)
    large = np.minimum(large, nb - 1)
    return (ret + np.where(n < max_exact, n, large)).astype(np.int32)


def _dil_bias_tables(rel_bias):
    qq = np.arange(DIL_QB)[:, None]
    kk = np.arange(DIL_KB)[None, :]
    ext = jnp.concatenate([rel_bias.astype(F32), jnp.full((1, rel_bias.shape[1]), NEG_INF, F32)], axis=0)
    tables = []
    for g, (_, r) in enumerate(DIL_PAIRS):
        idx = []
        for var in range(3):
            j = kk - qq - var * DIL_SIDE
            bucket = _t5_bucket(j * r)
            idx.append(np.where(np.abs(j) <= DIL_SIDE, bucket, NUM_BUCKETS))
        idx = np.stack(idx)
        tab = ext[idx][..., g * DIL_HEADS_PER_GROUP:(g + 1) * DIL_HEADS_PER_GROUP]
        tables.append(jnp.transpose(tab, (0, 3, 1, 2)))
    return tables


def _mem_kv_kernel(m_ref, g_ref, w_ref, o_ref):
    o_ref[...] = _dot(_rms(m_ref[...], g_ref[...]).astype(BF16), w_ref[...]).astype(BF16)


def _mem_kv(mem2, g, w):
    R, D = mem2.shape
    N = w.shape[1]
    return pl.pallas_call(
        _mem_kv_kernel,
        grid=(R // N_MEM,),
        in_specs=[
            pl.BlockSpec((N_MEM, D), lambda i: (i, 0)),
            pl.BlockSpec((1, D), lambda i: (0, 0)),
            pl.BlockSpec((D, N), lambda i: (0, 0)),
        ],
        out_specs=pl.BlockSpec((N_MEM, N), lambda i: (i, 0)),
        out_shape=jax.ShapeDtypeStruct((R, N), BF16),
        compiler_params=_cparams(1),
        name="mem_kv",
    )(mem2, g, w)


def _mem_attn_kernel(q_ref, kv_ref, o_ref):
    scale = X_HEAD_DIM ** -0.5
    hv = X_HEADS * X_HEAD_DIM
    for h in range(X_HEADS):
        cs = slice(h * X_HEAD_DIM, (h + 1) * X_HEAD_DIM)
        s = _dot_nt(q_ref[:, cs], kv_ref[:, cs]) * scale
        m = jnp.max(s, axis=-1, keepdims=True)
        p = jnp.exp(s - m)
        l = jnp.sum(p, axis=-1, keepdims=True)
        o = _dot(p.astype(BF16), kv_ref[:, hv + h * X_HEAD_DIM:hv + (h + 1) * X_HEAD_DIM]) / l
        o_ref[:, cs] = o.astype(BF16)


def _mem_attn(proj, kvm, *, S, tm=512):
    T = proj.shape[0]
    W = X_HEADS * X_HEAD_DIM
    nsb = S // tm
    return pl.pallas_call(
        _mem_attn_kernel,
        grid=(T // tm,),
        in_specs=[
            pl.BlockSpec((tm, W), lambda i: (i, _PROJ_XQ // W)),
            pl.BlockSpec((N_MEM, 2 * W), lambda i: (i // nsb, 0)),
        ],
        out_specs=pl.BlockSpec((tm, W), lambda i: (i, 0)),
        out_shape=jax.ShapeDtypeStruct((T, W), BF16),
        compiler_params=_cparams(1),
        name="mem_attn",
    )(proj, kvm)


def _merge_kernel(gate_ref, omla_ref, od0_ref, od1_ref, od2_ref, ls0_ref, ls1_ref, ls2_ref, omem_ref,
                  wmla_ref, wdil_ref, wmem_ref, o_ref):
    ls = [ls0_ref[...], ls1_ref[...], ls2_ref[...]]
    mx = jnp.maximum(jnp.maximum(ls[0], ls[1]), ls[2])
    es = [jnp.exp(x - mx) for x in ls]
    inv = 1.0 / (es[0] + es[1] + es[2])
    od = [od0_ref, od1_ref, od2_ref]
    parts = []
    for h in range(DIL_HEADS_PER_GROUP):
        cs = slice(h * DIL_HEAD_DIM, (h + 1) * DIL_HEAD_DIM)
        acc = None
        for gi in range(DIL_GROUPS):
            a = (es[gi] * inv)[:, h * 32:h * 32 + 1]
            term = a * od[gi][:, cs]
            acc = term if acc is None else acc + term
        parts.append(acc.astype(BF16))
    o_dil = jnp.concatenate(parts, axis=-1)

    def gate(b):
        return jax.nn.sigmoid(gate_ref[:, b * D_MODEL:(b + 1) * D_MODEL].astype(F32))

    merged = gate(0) * _dot(omla_ref[...], wmla_ref[...])
    merged += gate(1) * _dot(o_dil, wdil_ref[...])
    merged += gate(2) * _dot(omem_ref[...], wmem_ref[...])
    o_ref[...] = merged.astype(BF16)


def _merge(proj, o_mla, ods, lss, o_mem, wmla, wdil, wmem, *, tm=512):
    T = proj.shape[0]
    row = lambda i: (i, 0)
    const = lambda i: (0, 0)
    wspec = lambda w: pl.BlockSpec(w.shape, const, pipeline_mode=pl.Buffered(1))
    return pl.pallas_call(
        _merge_kernel,
        grid=(T // tm,),
        in_specs=[
            pl.BlockSpec((tm, N_BRANCH * D_MODEL), row),
            pl.BlockSpec((tm, o_mla.shape[1]), row),
            pl.BlockSpec((tm, DIL_W), row), pl.BlockSpec((tm, DIL_W), row), pl.BlockSpec((tm, DIL_W), row),
            pl.BlockSpec((tm, 128), row), pl.BlockSpec((tm, 128), row), pl.BlockSpec((tm, 128), row),
            pl.BlockSpec((tm, o_mem.shape[1]), row),
            wspec(wmla), wspec(wdil), wspec(wmem),
        ],
        out_specs=pl.BlockSpec((tm, D_MODEL), row),
        out_shape=jax.ShapeDtypeStruct((T, D_MODEL), BF16),
        compiler_params=_cparams(1),
        name="merge",
    )(proj, o_mla, *ods, *lss, o_mem, wmla, wdil, wmem)


def _out_mlp_kernel(x_ref, mg_ref, wout_ref, gm_ref, wup_ref, wdn_ref, gf_ref, o_ref, h_ref):
    j = pl.program_id(1)

    @pl.when(j == 0)
    def _():
        x1 = x_ref[...] + _dot(mg_ref[...], wout_ref[...])
        o_ref[...] = x1
        h_ref[...] = _rms(x1, gm_ref[...]).astype(BF16)

    u = jnp.maximum(_dot(h_ref[...], wup_ref[...]), 0.0)
    o_ref[...] += _dot((u * u).astype(BF16), wdn_ref[...])

    @pl.when(j == pl.num_programs(1) - 1)
    def _():
        o_ref[...] = _rms(o_ref[...], gf_ref[...])


def _out_mlp(x2, merged, wout, gm, wup, wdn, gf, *, tm=512, tf=1024):
    T, D = x2.shape
    row = lambda i, j: (i, 0)
    const = lambda i, j: (0, 0)
    return pl.pallas_call(
        _out_mlp_kernel,
        grid=(T // tm, D_FF // tf),
        in_specs=[
            pl.BlockSpec((tm, D), row),
            pl.BlockSpec((tm, D), row),
            pl.BlockSpec((D, D), const, pipeline_mode=pl.Buffered(1)),
            pl.BlockSpec((1, D), const),
            pl.BlockSpec((D, tf), lambda i, j: (0, j)),
            pl.BlockSpec((tf, D), lambda i, j: (j, 0)),
            pl.BlockSpec((1, D), const),
        ],
        out_specs=pl.BlockSpec((tm, D), row),
        out_shape=jax.ShapeDtypeStruct((T, D), F32),
        scratch_shapes=[pltpu.VMEM((tm, D), BF16)],
        compiler_params=_cparams(2),
        name="out_mlp",
    )(x2, merged, wout, gm, wup, wdn, gf)


def _rot_half_cols(w):
    half = QK_ROPE // 2
    return jnp.concatenate([-w[..., half:], w[..., :half]], axis=-1)


def _prep_weights(w_in, w_uq, w_ukv, w_mem_kv, w_b_mla, w_b_dil, w_b_mem, w_out, w_up, w_down):
    D = w_in.shape[0]
    o = 0
    cq = w_in[:, o:o + Q_LORA]; o += Q_LORA
    ckv = w_in[:, o:o + KV_LORA]; o += KV_LORA
    kr = w_in[:, o:o + QK_ROPE]; o += QK_ROPE
    dil = w_in[:, o:o + 3 * DIL_GROUPS * DIL_W]; o += 3 * DIL_GROUPS * DIL_W
    xq = w_in[:, o:o + X_HEADS * X_HEAD_DIM]; o += X_HEADS * X_HEAD_DIM
    gate = w_in[:, o:o + N_BRANCH * D_MODEL]
    z64 = jnp.zeros((D, 64), w_in.dtype)
    w_in_p = jnp.concatenate(
        [gate, xq, dil, cq, ckv, kr, z64, _rot_half_cols(kr), z64, jnp.zeros((D, 256), w_in.dtype)], axis=1).astype(BF16)

    wq = w_uq.reshape(Q_LORA, MLA_HEADS, QK_NOPE + QK_ROPE)
    nope, pe = wq[..., :QK_NOPE], wq[..., QK_NOPE:]
    zq = jnp.zeros((Q_LORA, MLA_HEADS, 64), w_uq.dtype)
    wqa = jnp.concatenate([nope, pe, zq], axis=-1).reshape(Q_LORA, MLA_HEADS * QK_PAD).astype(BF16)
    wqb = jnp.concatenate([_rot_half_cols(pe), zq], axis=-1).reshape(Q_LORA, MLA_HEADS * 128).astype(BF16)
    return dict(w_in=w_in_p, wqa=wqa, wqb=wqb, wkv=w_ukv.astype(BF16), wmkv=w_mem_kv.astype(BF16),
                wmla=w_b_mla.astype(BF16), wdil=w_b_dil.astype(BF16), wmem=w_b_mem.astype(BF16),
                wout=w_out.astype(BF16), wup=w_up.astype(BF16), wdn=w_down.astype(BF16))


def _rope_tables(S):
    half = QK_ROPE // 2
    inv = 1.0 / (ROPE_THETA ** (jnp.arange(half, dtype=F32) / half))
    ang = jnp.arange(S).astype(F32)[:, None] * inv[None, :]
    z = jnp.zeros((S, 64), F32)
    cos, sin = jnp.cos(ang), jnp.sin(ang)
    return jnp.concatenate([cos, cos, z], axis=1), jnp.concatenate([sin, sin, z], axis=1)


def _trunk(x, mem, W, bias_tabs, g_attn, g_qn, g_kvn, g_mem, g_mlp, g_final):
    B, S, D = x.shape
    T = B * S
    x2 = x.reshape(T, D)
    row = lambda g: g.reshape(1, -1).astype(F32)

    proj = _in_proj(x2, row(g_attn), W["w_in"])

    cos_t, sin_t = _rope_tables(S)
    q_scale = (QK_NOPE + QK_ROPE) ** -0.5 * math.log2(math.e)
    q, k, v = _mla_proj(proj, cos_t, sin_t, row(g_qn), row(g_kvn), W["wqa"], W["wqb"], W["wkv"], S=S, q_scale=q_scale)
    o_mla = _mla_flash(q, k, v, B=B, S=S)

    ods, lss = [], []
    for g, (_, r) in enumerate(DIL_PAIRS):
        od, ls = _dilated_group(proj.reshape(B, S // r, r * PROJ_W), bias_tabs[g], B=B, S=S, g=g, r=r)
        ods.append(od)
        lss.append(ls)

    kvm = _mem_kv(mem.reshape(B * N_MEM, D), row(g_mem), W["wmkv"])
    o_mem = _mem_attn(proj, kvm, S=S)

    merged = _merge(proj, o_mla, ods, lss, o_mem, W["wmla"], W["wdil"], W["wmem"])
    y = _out_mlp(x2, merged, W["wout"], row(g_mlp), W["wup"], W["wdn"], row(g_final))
    return y.reshape(B, S, D)


def kernel(x_prompt, x_sample, mem_prompt, mem_sample, rel_bias, g_attn, w_in, g_q_norm, w_uq, g_kv_norm, w_ukv,
           g_mem, w_mem_kv, w_b_mla, w_b_dil, w_b_mem, w_out, g_mlp, w_up, w_down, g_final):
    assert w_in.shape[0] == 1, "single layer"
    W = _prep_weights(w_in[0], w_uq[0], w_ukv[0], w_mem_kv[0], w_b_mla[0], w_b_dil[0], w_b_mem[0], w_out[0],
                      w_up[0], w_down[0])
    bias_tabs = _dil_bias_tables(rel_bias)
    args = (W, bias_tabs, g_attn[0], g_q_norm[0], g_kv_norm[0], g_mem[0], g_mlp[0], g_final)
    return (_trunk(x_prompt, mem_prompt, *args), _trunk(x_sample, mem_sample, *args))
```

```python
import functools
import math

import numpy as np
import jax
import jax.numpy as jnp
from jax import lax
from jax.experimental import pallas as pl
from jax.experimental.pallas import tpu as pltpu

F32 = jnp.float32
BF16 = jnp.bfloat16

D_MODEL = 2048
N_MEM = 256
MLA_HEADS = 12
Q_LORA = 512
KV_LORA = 512
QK_NOPE = 128
QK_ROPE = 64
V_HEAD = 128
ROPE_THETA = 10000.0
DIL_PAIRS = ((128, 1), (512, 4), (2048, 16))
DIL_GROUPS = 3
DIL_HEADS_PER_GROUP = 4
DIL_HEAD_DIM = 128
X_HEADS = 4
X_HEAD_DIM = 256
NUM_BUCKETS = 32
MAX_DISTANCE = 1024
D_FF = 4 * D_MODEL
N_BRANCH = 3
EPS = 1e-6
NEG_INF = -1e30

DIL_W = DIL_HEADS_PER_GROUP * DIL_HEAD_DIM
DIL_SIDE = 64
DIL_QB = 128
DIL_KB = DIL_QB + 2 * DIL_SIDE
QK_PAD = 256

_PROJ_GATE = 0
_PROJ_XQ = _PROJ_GATE + N_BRANCH * D_MODEL
_PROJ_DQ = _PROJ_XQ + X_HEADS * X_HEAD_DIM
_PROJ_DK = _PROJ_DQ + DIL_GROUPS * DIL_W
_PROJ_DV = _PROJ_DK + DIL_GROUPS * DIL_W
_PROJ_CQ = _PROJ_DV + DIL_GROUPS * DIL_W
_PROJ_CKV = _PROJ_CQ + Q_LORA
_PROJ_KR = _PROJ_CKV + KV_LORA
PROJ_W = _PROJ_KR + 512

VMEM_LIMIT = 52 * 1024 * 1024


def _cparams(n_axes):
    return pltpu.CompilerParams(dimension_semantics=("arbitrary",) * n_axes, vmem_limit_bytes=VMEM_LIMIT)


def _rms(x32, g32):
    return x32 * lax.rsqrt(jnp.mean(x32 * x32, axis=-1, keepdims=True) + EPS) * g32


def _dot(a, b):
    return jnp.dot(a, b, preferred_element_type=F32)


def _dot_nt(a, b):
    return lax.dot_general(a, b, (((1,), (1,)), ((), ())), preferred_element_type=F32)


def _in_proj_kernel(x_ref, g_ref, w_ref, o_ref, h_ref):
    @pl.when(pl.program_id(1) == 0)
    def _():
        h_ref[...] = _rms(x_ref[...], g_ref[...]).astype(BF16)

    o_ref[...] = _dot(h_ref[...], w_ref[...]).astype(BF16)


def _in_proj(x2, g, w, *, tm=1024, tn=1024):
    T, D = x2.shape
    return pl.pallas_call(
        _in_proj_kernel,
        grid=(T // tm, PROJ_W // tn),
        in_specs=[
            pl.BlockSpec((tm, D), lambda i, j: (i, 0)),
            pl.BlockSpec((1, D), lambda i, j: (0, 0)),
            pl.BlockSpec((D, tn), lambda i, j: (0, j)),
        ],
        out_specs=pl.BlockSpec((tm, tn), lambda i, j: (i, j)),
        out_shape=jax.ShapeDtypeStruct((T, PROJ_W), BF16),
        scratch_shapes=[pltpu.VMEM((tm, D), BF16)],
        compiler_params=_cparams(2),
        name="in_proj",
    )(x2, g, w)


def _mla_proj_kernel(cq_ref, ckv_ref, kr_ref, cos_ref, sin_ref, cost_ref, sint_ref, gq_ref, gkv_ref,
                     wqt_ref, wk_ref, wvt_ref, qt_ref, k_ref, vt_ref, *, q_scale):
    half = QK_ROPE // 2
    hd = QK_NOPE + QK_ROPE
    nq_t = _rms(cq_ref[...].astype(F32), gq_ref[...]).T.astype(BF16)
    q_t = _dot(wqt_ref[...], nq_t)
    cos_t = cost_ref[...]
    sin_t = sint_ref[...]
    zeros = jnp.zeros((QK_PAD - hd, q_t.shape[1]), BF16)
    for h in range(MLA_HEADS):
        r0, o0 = h * hd, h * QK_PAD
        x1 = q_t[r0 + QK_NOPE:r0 + QK_NOPE + half]
        x2 = q_t[r0 + QK_NOPE + half:r0 + hd]
        qt_ref[o0:o0 + QK_NOPE, :] = (q_t[r0:r0 + QK_NOPE] * q_scale).astype(BF16)
        qt_ref[o0 + QK_NOPE:o0 + QK_NOPE + half, :] = ((x1 * cos_t - x2 * sin_t) * q_scale).astype(BF16)
        qt_ref[o0 + QK_NOPE + half:o0 + hd, :] = ((x2 * cos_t + x1 * sin_t) * q_scale).astype(BF16)
        qt_ref[o0 + hd:o0 + QK_PAD, :] = zeros

    nkv = _rms(ckv_ref[...].astype(F32), gkv_ref[...])
    vt_ref[...] = _dot(wvt_ref[...], nkv.T.astype(BF16)).astype(BF16)
    kn = _dot(nkv.astype(BF16), wk_ref[...])
    kr = kr_ref[...].astype(F32)
    kpe = (kr[:, 0:128] * cos_ref[...] + kr[:, 128:256] * sin_ref[...]).astype(BF16)
    for h in range(MLA_HEADS):
        o0 = h * QK_PAD
        k_ref[:, o0:o0 + QK_NOPE] = kn[:, h * QK_NOPE:(h + 1) * QK_NOPE].astype(BF16)
        k_ref[:, o0 + QK_NOPE:o0 + QK_PAD] = kpe


def _mla_proj(proj, rope, gq, gkv, wqt, wk, wvt, *, S, q_scale, tm=512):
    T = proj.shape[0]
    nsb = S // tm
    HQ = MLA_HEADS * QK_PAD
    HV = MLA_HEADS * V_HEAD
    cos_k, sin_k, cos_t, sin_t = rope
    const = lambda i: (0, 0)
    wspec = lambda w: pl.BlockSpec(w.shape, const, pipeline_mode=pl.Buffered(1))
    return pl.pallas_call(
        functools.partial(_mla_proj_kernel, q_scale=q_scale),
        grid=(T // tm,),
        in_specs=[
            pl.BlockSpec((tm, Q_LORA), lambda i: (i, _PROJ_CQ // 512)),
            pl.BlockSpec((tm, KV_LORA), lambda i: (i, _PROJ_CKV // 512)),
            pl.BlockSpec((tm, 512), lambda i: (i, _PROJ_KR // 512)),
            pl.BlockSpec((tm, 128), lambda i: (i % nsb, 0)),
            pl.BlockSpec((tm, 128), lambda i: (i % nsb, 0)),
            pl.BlockSpec((QK_ROPE // 2, tm), lambda i: (0, i % nsb)),
            pl.BlockSpec((QK_ROPE // 2, tm), lambda i: (0, i % nsb)),
            pl.BlockSpec((1, Q_LORA), const),
            pl.BlockSpec((1, KV_LORA), const),
            wspec(wqt), wspec(wk), wspec(wvt),
        ],
        out_specs=[
            pl.BlockSpec((HQ, tm), lambda i: (0, i)),
            pl.BlockSpec((tm, HQ), lambda i: (i, 0)),
            pl.BlockSpec((HV, tm), lambda i: (0, i)),
        ],
        out_shape=[
            jax.ShapeDtypeStruct((HQ, T), BF16),
            jax.ShapeDtypeStruct((T, HQ), BF16),
            jax.ShapeDtypeStruct((HV, T), BF16),
        ],
        compiler_params=_cparams(1),
        name="mla_proj",
    )(proj, proj, proj, cos_k, sin_k, cos_t, sin_t, gq, gkv, wqt, wk, wvt)


FLASH_SLAB = 32
FLASH_QK_ROWS = 256


def _mla_flash_kernel(qt_ref, k_ref, vt_ref, o_ref, s0_ref, s1_ref, p0_ref, p1_ref, acc_ref, *, tk, nk):
    tq = qt_ref.shape[1]
    qt = qt_ref[...]
    s_refs = (s0_ref, s1_ref)
    p_refs = (p0_ref, p1_ref)

    def stage1(j, slot):
        r0 = pl.multiple_of(j * tk, tk)
        mx = None
        for q0 in range(0, tk, FLASH_QK_ROWS):
            s = _dot(k_ref[pl.ds(r0 + q0, FLASH_QK_ROWS), :], qt)
            s_refs[slot][q0:q0 + FLASH_QK_ROWS, :] = s
            part = jnp.max(s.reshape(FLASH_QK_ROWS // FLASH_SLAB, FLASH_SLAB, tq), axis=0)
            mx = part if mx is None else jnp.maximum(mx, part)
        return jnp.max(mx, axis=0, keepdims=True)

    def stage2(slot, m, l, cmax):
        m_new = jnp.maximum(m, cmax)
        alpha = jnp.exp2(m - m_new)
        mb = jnp.broadcast_to(m_new, (FLASH_SLAB, tq))
        ps = None
        for r in range(0, tk, FLASH_SLAB):
            p = jnp.exp2(s_refs[slot][r:r + FLASH_SLAB, :] - mb)
            p_refs[slot][r:r + FLASH_SLAB, :] = p.astype(BF16)
            ps = p if ps is None else ps + p
        return m_new, alpha * l + jnp.sum(ps, axis=0, keepdims=True), alpha

    def stage3(j, slot, alpha):
        r0 = pl.multiple_of(j * tk, tk)
        acc_ref[...] = alpha * acc_ref[...] + _dot(vt_ref[:, pl.ds(r0, tk)], p_refs[slot][...])

    acc_ref[...] = jnp.zeros_like(acc_ref)
    m = jnp.full((1, tq), -jnp.inf, F32)
    l = jnp.zeros((1, tq), F32)
    c0 = stage1(0, 0)
    c1 = stage1(1, 1)
    m, l, a0 = stage2(0, m, l, c0)

    def body(i, carry):
        m, l, a_prev, c_next = carry
        j = 2 * i
        c_a = stage1(j + 2, 0)
        m, l, a_cur = stage2(1, m, l, c_next)
        stage3(j, 0, a_prev)
        c_b = stage1(j + 3, 1)
        m, l, a_nxt = stage2(0, m, l, c_a)
        stage3(j + 1, 1, a_cur)
        return m, l, a_nxt, c_b

    m, l, a_prev, c_next = lax.fori_loop(0, (nk - 2) // 2, body, (m, l, a0, c1))
    m, l, a_cur = stage2(1, m, l, c_next)
    stage3(nk - 2, 0, a_prev)
    stage3(nk - 1, 1, a_cur)
    o_ref[...] = (acc_ref[...] / l).T.astype(BF16)


def _mla_flash(qt, k, vt, *, B, S, tq=512, tk=1024):
    T = B * S
    nqb = S // tq
    nk = S // tk
    assert nk >= 2 and nk % 2 == 0
    return pl.pallas_call(
        functools.partial(_mla_flash_kernel, tk=tk, nk=nk),
        grid=(B, MLA_HEADS, nqb),
        in_specs=[
            pl.BlockSpec((QK_PAD, tq), lambda b, h, i: (h, b * nqb + i)),
            pl.BlockSpec((S, QK_PAD), lambda b, h, i: (b, h)),
            pl.BlockSpec((V_HEAD, S), lambda b, h, i: (h, b)),
        ],
        out_specs=pl.BlockSpec((tq, V_HEAD), lambda b, h, i: (b * nqb + i, h)),
        out_shape=jax.ShapeDtypeStruct((T, MLA_HEADS * V_HEAD), BF16),
        scratch_shapes=[pltpu.VMEM((tk, tq), F32), pltpu.VMEM((tk, tq), F32),
                        pltpu.VMEM((tk, tq), BF16), pltpu.VMEM((tk, tq), BF16),
                        pltpu.VMEM((V_HEAD, tq), F32)],
        compiler_params=_cparams(3),
        name="mla_flash",
    )(qt, k, vt)


def _dil_kernel(q_ref, k_ref, v_ref, bias_ref, o_ref, lse_ref, *, L, ta):
    scale = DIL_HEAD_DIM ** -0.5
    lane = lax.broadcasted_iota(jnp.int32, (DIL_QB, 128), 1)
    base = pl.program_id(2) * ta
    for t in range(ta // DIL_QB):
        a0 = base + t * DIL_QB
        k0 = jnp.clip(a0 - DIL_SIDE, 0, L - DIL_KB)
        var = (a0 - k0) // DIL_SIDE
        k0 = pl.multiple_of(k0, DIL_SIDE)
        rows = slice(t * DIL_QB, (t + 1) * DIL_QB)
        lses = []
        for h in range(DIL_HEADS_PER_GROUP):
            cs = slice(h * DIL_HEAD_DIM, (h + 1) * DIL_HEAD_DIM)
            qh = q_ref[0, 0, rows, cs]
            kh = k_ref[0, 0, pl.ds(k0, DIL_KB), cs]
            vh = v_ref[0, 0, pl.ds(k0, DIL_KB), cs]
            s = _dot_nt(qh, kh) * scale + bias_ref[var, h]
            m = jnp.max(s, axis=-1, keepdims=True)
            p = jnp.exp(s - m)
            l = jnp.sum(p, axis=-1, keepdims=True)
            o_ref[0, 0, rows, cs] = _dot(p.astype(BF16), vh) / l
            lses.append(m + jnp.log(l))
        packed = jnp.where(lane < 32, lses[0], jnp.where(lane < 64, lses[1], jnp.where(lane < 96, lses[2], lses[3])))
        lse_ref[0, 0, rows, :] = packed


def _dilated_group(q, k, v, bias, *, qcol, kcol, vcol, name):
    B, r, L, _ = q.shape
    ta = min(L, 512)
    return pl.pallas_call(
        functools.partial(_dil_kernel, L=L, ta=ta),
        grid=(B, r, L // ta),
        in_specs=[
            pl.BlockSpec((1, 1, ta, DIL_W), lambda b, c, a: (b, c, a, qcol)),
            pl.BlockSpec((1, 1, L, DIL_W), lambda b, c, a: (b, c, 0, kcol)),
            pl.BlockSpec((1, 1, L, DIL_W), lambda b, c, a: (b, c, 0, vcol)),
            pl.BlockSpec(bias.shape, lambda b, c, a: (0, 0, 0, 0)),
        ],
        out_specs=[
            pl.BlockSpec((1, 1, ta, DIL_W), lambda b, c, a: (b, c, a, 0)),
            pl.BlockSpec((1, 1, ta, 128), lambda b, c, a: (b, c, a, 0)),
        ],
        out_shape=[
            jax.ShapeDtypeStruct((B, r, L, DIL_W), F32),
            jax.ShapeDtypeStruct((B, r, L, 128), F32),
        ],
        compiler_params=_cparams(3),
        name=name,
    )(q, k, v, bias)


def _dilated(proj, bias_tabs, *, B, S):
    T = B * S
    ods, lss = [], []
    for g, (_, r) in enumerate(DIL_PAIRS):
        L = S // r
        if r == 1:
            p4 = proj.reshape(B, 1, S, PROJ_W)
            od, ls = _dilated_group(p4, p4, p4, bias_tabs[g], qcol=_PROJ_DQ // DIL_W + g, kcol=_PROJ_DK // DIL_W + g,
                                    vcol=_PROJ_DV // DIL_W + g, name=f"dilated_g{g}")
        else:
            def stream(c0):
                x = lax.slice_in_dim(proj, c0 + g * DIL_W, c0 + (g + 1) * DIL_W, axis=1)
                return x.reshape(B, L, r, DIL_W).transpose(0, 2, 1, 3)
            od, ls = _dilated_group(stream(_PROJ_DQ), stream(_PROJ_DK), stream(_PROJ_DV), bias_tabs[g],
                                    qcol=0, kcol=0, vcol=0, name=f"dilated_g{g}")
            od = od.transpose(0, 2, 1, 3)
            ls = ls.transpose(0, 2, 1, 3)
        ods.append(od.reshape(T, DIL_W))
        lss.append(ls.reshape(T, 128))
    return ods, lss


def _t5_bucket(rel):
    nb = NUM_BUCKETS // 2
    ret = (rel > 0).astype(np.int32) * nb
    n = np.abs(rel)
    max_exact = nb // 2
    large = max_exact + (np.log(np.maximum(n, 1) / max_exact) / np.log(MAX_DISTANCE / max_exact)
                         * (nb - max_exact)).astype(np.int32)
    large = np.minimum(large, nb - 1)
    return (ret + np.where(n < max_exact, n, large)).astype(np.int32)


def _dil_bias_tables(rel_bias):
    qq = np.arange(DIL_QB)[:, None]
    kk = np.arange(DIL_KB)[None, :]
    ext = jnp.concatenate([rel_bias.astype(F32), jnp.full((1, rel_bias.shape[1]), NEG_INF, F32)], axis=0)
    tables = []
    for g, (_, r) in enumerate(DIL_PAIRS):
        idx = []
        for var in range(3):
            j = kk - qq - var * DIL_SIDE
            bucket = _t5_bucket(j * r)
            idx.append(np.where(np.abs(j) <= DIL_SIDE, bucket, NUM_BUCKETS))
        onehot = (np.stack(idx)[..., None] == np.arange(NUM_BUCKETS + 1)).astype(np.float32)
        heads = ext[:, g * DIL_HEADS_PER_GROUP:(g + 1) * DIL_HEADS_PER_GROUP]
        tables.append(jnp.einsum("vqkn,nh->vhqk", onehot, heads, precision=lax.Precision.HIGHEST))
    return tables


def _mem_kv_kernel(m_ref, g_ref, w_ref, o_ref):
    o_ref[...] = _dot(_rms(m_ref[...], g_ref[...]).astype(BF16), w_ref[...]).astype(BF16)


def _mem_kv(mem2, g, w):
    R, D = mem2.shape
    N = w.shape[1]
    return pl.pallas_call(
        _mem_kv_kernel,
        grid=(R // N_MEM,),
        in_specs=[
            pl.BlockSpec((N_MEM, D), lambda i: (i, 0)),
            pl.BlockSpec((1, D), lambda i: (0, 0)),
            pl.BlockSpec((D, N), lambda i: (0, 0)),
        ],
        out_specs=pl.BlockSpec((N_MEM, N), lambda i: (i, 0)),
        out_shape=jax.ShapeDtypeStruct((R, N), BF16),
        compiler_params=_cparams(1),
        name="mem_kv",
    )(mem2, g, w)


def _mem_attn_kernel(q_ref, kv_ref, o_ref):
    scale = X_HEAD_DIM ** -0.5
    hv = X_HEADS * X_HEAD_DIM
    for h in range(X_HEADS):
        cs = slice(h * X_HEAD_DIM, (h + 1) * X_HEAD_DIM)
        s = _dot_nt(q_ref[:, cs], kv_ref[:, cs]) * scale
        m = jnp.max(s, axis=-1, keepdims=True)
        p = jnp.exp(s - m)
        l = jnp.sum(p, axis=-1, keepdims=True)
        o = _dot(p.astype(BF16), kv_ref[:, hv + h * X_HEAD_DIM:hv + (h + 1) * X_HEAD_DIM]) / l
        o_ref[:, cs] = o.astype(BF16)


def _mem_attn(proj, kvm, *, S, tm=512):
    T = proj.shape[0]
    W = X_HEADS * X_HEAD_DIM
    nsb = S // tm
    return pl.pallas_call(
        _mem_attn_kernel,
        grid=(T // tm,),
        in_specs=[
            pl.BlockSpec((tm, W), lambda i: (i, _PROJ_XQ // W)),
            pl.BlockSpec((N_MEM, 2 * W), lambda i: (i // nsb, 0)),
        ],
        out_specs=pl.BlockSpec((tm, W), lambda i: (i, 0)),
        out_shape=jax.ShapeDtypeStruct((T, W), BF16),
        compiler_params=_cparams(1),
        name="mem_attn",
    )(proj, kvm)


def _merge_kernel(gate_ref, omla_ref, od0_ref, od1_ref, od2_ref, ls0_ref, ls1_ref, ls2_ref, omem_ref,
                  wmla_ref, wdil_ref, wmem_ref, o_ref):
    ls = [ls0_ref[...], ls1_ref[...], ls2_ref[...]]
    mx = jnp.maximum(jnp.maximum(ls[0], ls[1]), ls[2])
    es = [jnp.exp(x - mx) for x in ls]
    inv = 1.0 / (es[0] + es[1] + es[2])
    od = [od0_ref, od1_ref, od2_ref]
    parts = []
    for h in range(DIL_HEADS_PER_GROUP):
        cs = slice(h * DIL_HEAD_DIM, (h + 1) * DIL_HEAD_DIM)
        acc = None
        for gi in range(DIL_GROUPS):
            a = (es[gi] * inv)[:, h * 32:h * 32 + 1]
            term = a * od[gi][:, cs]
            acc = term if acc is None else acc + term
        parts.append(acc.astype(BF16))
    o_dil = jnp.concatenate(parts, axis=-1)

    def gate(b):
        return jax.nn.sigmoid(gate_ref[:, b * D_MODEL:(b + 1) * D_MODEL].astype(F32))

    merged = gate(0) * _dot(omla_ref[...], wmla_ref[...])
    merged += gate(1) * _dot(o_dil, wdil_ref[...])
    merged += gate(2) * _dot(omem_ref[...], wmem_ref[...])
    o_ref[...] = merged.astype(BF16)


def _merge(proj, o_mla, ods, lss, o_mem, wmla, wdil, wmem, *, tm=512):
    T = proj.shape[0]
    row = lambda i: (i, 0)
    const = lambda i: (0, 0)
    wspec = lambda w: pl.BlockSpec(w.shape, const, pipeline_mode=pl.Buffered(1))
    return pl.pallas_call(
        _merge_kernel,
        grid=(T // tm,),
        in_specs=[
            pl.BlockSpec((tm, N_BRANCH * D_MODEL), row),
            pl.BlockSpec((tm, o_mla.shape[1]), row),
            pl.BlockSpec((tm, DIL_W), row), pl.BlockSpec((tm, DIL_W), row), pl.BlockSpec((tm, DIL_W), row),
            pl.BlockSpec((tm, 128), row), pl.BlockSpec((tm, 128), row), pl.BlockSpec((tm, 128), row),
            pl.BlockSpec((tm, o_mem.shape[1]), row),
            wspec(wmla), wspec(wdil), wspec(wmem),
        ],
        out_specs=pl.BlockSpec((tm, D_MODEL), row),
        out_shape=jax.ShapeDtypeStruct((T, D_MODEL), BF16),
        compiler_params=_cparams(1),
        name="merge",
    )(proj, o_mla, *ods, *lss, o_mem, wmla, wdil, wmem)


def _out_mlp_kernel(x_ref, mg_ref, wout_ref, gm_ref, wup_ref, wdn_ref, gf_ref, o_ref, h_ref):
    j = pl.program_id(1)

    @pl.when(j == 0)
    def _():
        x1 = x_ref[...] + _dot(mg_ref[...], wout_ref[...])
        o_ref[...] = x1
        h_ref[...] = _rms(x1, gm_ref[...]).astype(BF16)

    u = jnp.maximum(_dot(h_ref[...], wup_ref[...]), 0.0)
    o_ref[...] += _dot((u * u).astype(BF16), wdn_ref[...])

    @pl.when(j == pl.num_programs(1) - 1)
    def _():
        o_ref[...] = _rms(o_ref[...], gf_ref[...])


def _out_mlp(x2, merged, wout, gm, wup, wdn, gf, *, tm=512, tf=1024):
    T, D = x2.shape
    row = lambda i, j: (i, 0)
    const = lambda i, j: (0, 0)
    return pl.pallas_call(
        _out_mlp_kernel,
        grid=(T // tm, D_FF // tf),
        in_specs=[
            pl.BlockSpec((tm, D), row),
            pl.BlockSpec((tm, D), row),
            pl.BlockSpec((D, D), const, pipeline_mode=pl.Buffered(1)),
            pl.BlockSpec((1, D), const),
            pl.BlockSpec((D, tf), lambda i, j: (0, j)),
            pl.BlockSpec((tf, D), lambda i, j: (j, 0)),
            pl.BlockSpec((1, D), const),
        ],
        out_specs=pl.BlockSpec((tm, D), row),
        out_shape=jax.ShapeDtypeStruct((T, D), F32),
        scratch_shapes=[pltpu.VMEM((tm, D), BF16)],
        compiler_params=_cparams(2),
        name="out_mlp",
    )(x2, merged, wout, gm, wup, wdn, gf)


def _rot_half_cols(w):
    half = QK_ROPE // 2
    return jnp.concatenate([-w[..., half:], w[..., :half]], axis=-1)


def _prep_weights(w_in, w_uq, w_ukv, w_mem_kv, w_b_mla, w_b_dil, w_b_mem, w_out, w_up, w_down):
    D = w_in.shape[0]
    o = 0
    cq = w_in[:, o:o + Q_LORA]; o += Q_LORA
    ckv = w_in[:, o:o + KV_LORA]; o += KV_LORA
    kr = w_in[:, o:o + QK_ROPE]; o += QK_ROPE
    dil = w_in[:, o:o + 3 * DIL_GROUPS * DIL_W]; o += 3 * DIL_GROUPS * DIL_W
    xq = w_in[:, o:o + X_HEADS * X_HEAD_DIM]; o += X_HEADS * X_HEAD_DIM
    gate = w_in[:, o:o + N_BRANCH * D_MODEL]
    z64 = jnp.zeros((D, 64), w_in.dtype)
    w_in_p = jnp.concatenate(
        [gate, xq, dil, cq, ckv, kr, z64, _rot_half_cols(kr), z64, jnp.zeros((D, 256), w_in.dtype)], axis=1).astype(BF16)

    wkv = w_ukv.reshape(KV_LORA, MLA_HEADS, QK_NOPE + V_HEAD)
    wk = wkv[..., :QK_NOPE].reshape(KV_LORA, MLA_HEADS * QK_NOPE).astype(BF16)
    wvt = wkv[..., QK_NOPE:].reshape(KV_LORA, MLA_HEADS * V_HEAD).T.astype(BF16)
    return dict(w_in=w_in_p, wqt=w_uq.T.astype(BF16), wk=wk, wvt=wvt, wmkv=w_mem_kv.astype(BF16),
                wmla=w_b_mla.astype(BF16), wdil=w_b_dil.astype(BF16), wmem=w_b_mem.astype(BF16),
                wout=w_out.astype(BF16), wup=w_up.astype(BF16), wdn=w_down.astype(BF16))


def _rope_tables(S):
    half = QK_ROPE // 2
    inv = 1.0 / (ROPE_THETA ** (jnp.arange(half, dtype=F32) / half))
    ang = jnp.arange(S).astype(F32)[:, None] * inv[None, :]
    z = jnp.zeros((S, 64), F32)
    cos, sin = jnp.cos(ang), jnp.sin(ang)
    return jnp.concatenate([cos, cos, z], axis=1), jnp.concatenate([sin, sin, z], axis=1), cos.T, sin.T


def _trunk(x, mem, W, bias_tabs, g_attn, g_qn, g_kvn, g_mem, g_mlp, g_final):
    B, S, D = x.shape
    T = B * S
    x2 = x.reshape(T, D)
    row = lambda g: g.reshape(1, -1).astype(F32)

    proj = _in_proj(x2, row(g_attn), W["w_in"])

    q_scale = (QK_NOPE + QK_ROPE) ** -0.5 * math.log2(math.e)
    qt, k, vt = _mla_proj(proj, _rope_tables(S), row(g_qn), row(g_kvn), W["wqt"], W["wk"], W["wvt"], S=S,
                          q_scale=q_scale)
    o_mla = _mla_flash(qt, k, vt, B=B, S=S)

    ods, lss = _dilated(proj, bias_tabs, B=B, S=S)

    kvm = _mem_kv(mem.reshape(B * N_MEM, D), row(g_mem), W["wmkv"])
    o_mem = _mem_attn(proj, kvm, S=S)

    merged = _merge(proj, o_mla, ods, lss, o_mem, W["wmla"], W["wdil"], W["wmem"])
    y = _out_mlp(x2, merged, W["wout"], row(g_mlp), W["wup"], W["wdn"], row(g_final))
    return y.reshape(B, S, D)


def kernel(x_prompt, x_sample, mem_prompt, mem_sample, rel_bias, g_attn, w_in, g_q_norm, w_uq, g_kv_norm, w_ukv,
           g_mem, w_mem_kv, w_b_mla, w_b_dil, w_b_mem, w_out, g_mlp, w_up, w_down, g_final):
    assert w_in.shape[0] == 1, "single layer"
    W = _prep_weights(w_in[0], w_uq[0], w_ukv[0], w_mem_kv[0], w_b_mla[0], w_b_dil[0], w_b_mem[0], w_out[0],
                      w_up[0], w_down[0])
    bias_tabs = _dil_bias_tables(rel_bias)
    args = (W, bias_tabs, g_attn[0], g_q_norm[0], g_kv_norm[0], g_mem[0], g_mlp[0], g_final)
    return (_trunk(x_prompt, mem_prompt, *args), _trunk(x_sample, mem_sample, *args))
```

```python
import functools
import math

import numpy as np
import jax
import jax.numpy as jnp
from jax import lax
from jax.experimental import pallas as pl
from jax.experimental.pallas import tpu as pltpu

F32 = jnp.float32
BF16 = jnp.bfloat16

D_MODEL = 2048
N_MEM = 256
MLA_HEADS = 12
Q_LORA = 512
KV_LORA = 512
QK_NOPE = 128
QK_ROPE = 64
V_HEAD = 128
ROPE_THETA = 10000.0
DIL_PAIRS = ((128, 1), (512, 4), (2048, 16))
DIL_GROUPS = 3
DIL_HEADS_PER_GROUP = 4
DIL_HEAD_DIM = 128
X_HEADS = 4
X_HEAD_DIM = 256
NUM_BUCKETS = 32
MAX_DISTANCE = 1024
D_FF = 4 * D_MODEL
N_BRANCH = 3
EPS = 1e-6
NEG_INF = -1e30

DIL_W = DIL_HEADS_PER_GROUP * DIL_HEAD_DIM
DIL_SIDE = 64
DIL_QB = 128
DIL_KB = DIL_QB + 2 * DIL_SIDE
QK_PAD = 256
VT_ROWS = V_HEAD + 16

_PROJ_GATE = 0
_PROJ_XQ = _PROJ_GATE + N_BRANCH * D_MODEL
_PROJ_DQ = _PROJ_XQ + X_HEADS * X_HEAD_DIM
_PROJ_DK = _PROJ_DQ + DIL_GROUPS * DIL_W
_PROJ_DV = _PROJ_DK + DIL_GROUPS * DIL_W
_PROJ_CQ = _PROJ_DV + DIL_GROUPS * DIL_W
_PROJ_CKV = _PROJ_CQ + Q_LORA
_PROJ_KR = _PROJ_CKV + KV_LORA
PROJ_W = _PROJ_KR + 512

VMEM_LIMIT = 52 * 1024 * 1024


def _cparams(n_axes):
    return pltpu.CompilerParams(dimension_semantics=("arbitrary",) * n_axes, vmem_limit_bytes=VMEM_LIMIT)


def _rms(x32, g32):
    return x32 * lax.rsqrt(jnp.mean(x32 * x32, axis=-1, keepdims=True) + EPS) * g32


def _dot(a, b):
    return jnp.dot(a, b, preferred_element_type=F32)


def _dot_nt(a, b):
    return lax.dot_general(a, b, (((1,), (1,)), ((), ())), preferred_element_type=F32)


def _in_proj_kernel(x_ref, g_ref, w_ref, o_ref, h_ref):
    @pl.when(pl.program_id(1) == 0)
    def _():
        h_ref[...] = _rms(x_ref[...], g_ref[...]).astype(BF16)

    o_ref[...] = _dot(h_ref[...], w_ref[...]).astype(BF16)


def _in_proj(x2, g, w, *, tm=1024, tn=1024):
    T, D = x2.shape
    return pl.pallas_call(
        _in_proj_kernel,
        grid=(T // tm, PROJ_W // tn),
        in_specs=[
            pl.BlockSpec((tm, D), lambda i, j: (i, 0)),
            pl.BlockSpec((1, D), lambda i, j: (0, 0)),
            pl.BlockSpec((D, tn), lambda i, j: (0, j)),
        ],
        out_specs=pl.BlockSpec((tm, tn), lambda i, j: (i, j)),
        out_shape=jax.ShapeDtypeStruct((T, PROJ_W), BF16),
        scratch_shapes=[pltpu.VMEM((tm, D), BF16)],
        compiler_params=_cparams(2),
        name="in_proj",
    )(x2, g, w)


def _mla_proj_kernel(cq_ref, ckv_ref, kr_ref, cos_ref, sin_ref, cost_ref, sint_ref, gq_ref, gkv_ref,
                     wqt_ref, wk_ref, wvt_ref, qt_ref, k_ref, vt_ref, *, q_scale):
    half = QK_ROPE // 2
    hd = QK_NOPE + QK_ROPE
    nq_t = _rms(cq_ref[...].astype(F32), gq_ref[...]).T.astype(BF16)
    q_t = _dot(wqt_ref[...], nq_t)
    cos_t = cost_ref[...]
    sin_t = sint_ref[...]
    zeros = jnp.zeros((QK_PAD - hd, q_t.shape[1]), BF16)
    for h in range(MLA_HEADS):
        r0, o0 = h * hd, h * QK_PAD
        x1 = q_t[r0 + QK_NOPE:r0 + QK_NOPE + half]
        x2 = q_t[r0 + QK_NOPE + half:r0 + hd]
        qt_ref[o0:o0 + QK_NOPE, :] = (q_t[r0:r0 + QK_NOPE] * q_scale).astype(BF16)
        qt_ref[o0 + QK_NOPE:o0 + QK_NOPE + half, :] = ((x1 * cos_t - x2 * sin_t) * q_scale).astype(BF16)
        qt_ref[o0 + QK_NOPE + half:o0 + hd, :] = ((x2 * cos_t + x1 * sin_t) * q_scale).astype(BF16)
        qt_ref[o0 + hd:o0 + QK_PAD, :] = zeros

    nkv = _rms(ckv_ref[...].astype(F32), gkv_ref[...])
    v_t = _dot(wvt_ref[...], nkv.T.astype(BF16)).astype(BF16)
    extra = jnp.where(lax.broadcasted_iota(jnp.int32, (VT_ROWS - V_HEAD, v_t.shape[1]), 0) == 0, 1.0, 0.0).astype(BF16)
    for h in range(MLA_HEADS):
        vt_ref[h * VT_ROWS:h * VT_ROWS + V_HEAD, :] = v_t[h * V_HEAD:(h + 1) * V_HEAD]
        vt_ref[h * VT_ROWS + V_HEAD:(h + 1) * VT_ROWS, :] = extra
    kn = _dot(nkv.astype(BF16), wk_ref[...])
    kr = kr_ref[...].astype(F32)
    kpe = (kr[:, 0:128] * cos_ref[...] + kr[:, 128:256] * sin_ref[...]).astype(BF16)
    for h in range(MLA_HEADS):
        o0 = h * QK_PAD
        k_ref[:, o0:o0 + QK_NOPE] = kn[:, h * QK_NOPE:(h + 1) * QK_NOPE].astype(BF16)
        k_ref[:, o0 + QK_NOPE:o0 + QK_PAD] = kpe


def _mla_proj(proj, rope, gq, gkv, wqt, wk, wvt, *, S, q_scale, tm=512):
    T = proj.shape[0]
    nsb = S // tm
    HQ = MLA_HEADS * QK_PAD
    HV = MLA_HEADS * VT_ROWS
    cos_k, sin_k, cos_t, sin_t = rope
    const = lambda i: (0, 0)
    wspec = lambda w: pl.BlockSpec(w.shape, const, pipeline_mode=pl.Buffered(1))
    return pl.pallas_call(
        functools.partial(_mla_proj_kernel, q_scale=q_scale),
        grid=(T // tm,),
        in_specs=[
            pl.BlockSpec((tm, Q_LORA), lambda i: (i, _PROJ_CQ // 512)),
            pl.BlockSpec((tm, KV_LORA), lambda i: (i, _PROJ_CKV // 512)),
            pl.BlockSpec((tm, 512), lambda i: (i, _PROJ_KR // 512)),
            pl.BlockSpec((tm, 128), lambda i: (i % nsb, 0)),
            pl.BlockSpec((tm, 128), lambda i: (i % nsb, 0)),
            pl.BlockSpec((QK_ROPE // 2, tm), lambda i: (0, i % nsb)),
            pl.BlockSpec((QK_ROPE // 2, tm), lambda i: (0, i % nsb)),
            pl.BlockSpec((1, Q_LORA), const),
            pl.BlockSpec((1, KV_LORA), const),
            wspec(wqt), wspec(wk), wspec(wvt),
        ],
        out_specs=[
            pl.BlockSpec((HQ, tm), lambda i: (0, i)),
            pl.BlockSpec((tm, HQ), lambda i: (i, 0)),
            pl.BlockSpec((HV, tm), lambda i: (0, i)),
        ],
        out_shape=[
            jax.ShapeDtypeStruct((HQ, T), BF16),
            jax.ShapeDtypeStruct((T, HQ), BF16),
            jax.ShapeDtypeStruct((HV, T), BF16),
        ],
        compiler_params=_cparams(1),
        name="mla_proj",
    )(proj, proj, proj, cos_k, sin_k, cos_t, sin_t, gq, gkv, wqt, wk, wvt)


FLASH_SLAB = 32
FLASH_QK_ROWS = 256


FLASH_LAZY = 60.0
FLASH_PV_AFTER = 4


def _zero_like_bits(v):
    u = pltpu.bitcast(v, jnp.uint32)
    return pltpu.bitcast(lax.shift_right_logical(lax.shift_right_logical(u, jnp.uint32(16)), jnp.uint32(16)), F32)


def _mla_flash_kernel(qt_ref, k_ref, vt_ref, o_ref, p0_ref, p1_ref, acc_ref, *, tq, tk, nk, nq):
    p_refs = (p0_ref, p1_ref)
    nc = nq * nk

    def pv(c, slot):
        qi = c // nk
        first = c == qi * nk
        r0 = pl.multiple_of((c - qi * nk) * tk, tk)
        acc = jnp.where(first, 0.0, acc_ref[...]) + _dot(vt_ref[:, pl.ds(r0, tk)], p_refs[slot][...])
        acc_ref[...] = acc
        return acc[0:1]

    def qk_exp(c, slot, ref, pv_args=None):
        qi = c // nk
        first = c == qi * nk
        r0 = pl.multiple_of((c - qi * nk) * tk, tk)
        c0 = pl.multiple_of(qi * tq, tq)
        mx = None
        shift = ref
        for n, q0 in enumerate(range(0, tk, FLASH_QK_ROWS)):
            if n == FLASH_PV_AFTER and pv_args is not None:
                shift = ref + _zero_like_bits(pv(*pv_args))
            s = _dot(k_ref[pl.ds(r0 + q0, FLASH_QK_ROWS), :], qt_ref[:, pl.ds(c0, tq)])
            p_refs[slot][q0:q0 + FLASH_QK_ROWS, :] = jnp.exp2((s - shift).astype(BF16))
            part = jnp.max(s.reshape(FLASH_QK_ROWS // FLASH_SLAB, FLASH_SLAB, tq), axis=0)
            mx = part if mx is None else jnp.maximum(mx, part)
        cmax = jnp.max(mx, axis=0, keepdims=True)
        d = cmax - ref
        bad = jnp.logical_or(jnp.max(d) > FLASH_LAZY, jnp.logical_and(first, jnp.min(d) < -FLASH_LAZY))
        return cmax, bad

    def fix(c, slot, ref, cmax, bad):
        def redo():
            first = c % nk == 0
            new = jnp.where(first, cmax, jnp.maximum(ref, cmax))
            acc_ref[...] = acc_ref[...] * jnp.exp2(ref - new)
            qk_exp(c, slot, new)
            return new
        return lax.cond(bad, redo, lambda: ref)

    def finalize(c):
        qi = c // nk
        acc = acc_ref[...]
        o = acc[:V_HEAD] * (1.0 / acc[V_HEAD:V_HEAD + 1])
        o_ref[pl.ds(pl.multiple_of(qi * tq, tq), tq), :] = o.T.astype(BF16)

    def maybe_finalize(c):
        @pl.when(c % nk == nk - 1)
        def _():
            finalize(c)

    acc_ref[...] = jnp.zeros_like(acc_ref)
    ref = jnp.full((1, tq), -jnp.inf, F32)
    cmax, bad = qk_exp(0, 0, ref)
    ref = fix(0, 0, ref, cmax, bad)

    def body(i, ref):
        c = 2 * i
        cmax, bad = qk_exp(c + 1, 1, ref, (c, 0))
        maybe_finalize(c)
        ref = fix(c + 1, 1, ref, cmax, bad)
        cmax, bad = qk_exp(c + 2, 0, ref, (c + 1, 1))
        maybe_finalize(c + 1)
        ref = fix(c + 2, 0, ref, cmax, bad)
        return ref

    ref = lax.fori_loop(0, (nc - 2) // 2, body, ref)
    cmax, bad = qk_exp(nc - 1, 1, ref, (nc - 2, 0))
    ref = fix(nc - 1, 1, ref, cmax, bad)
    pv(nc - 1, 1)
    finalize(nc - 1)


def _mla_flash(qt, k, vt, *, B, S, tq=512, tk=2048):
    T = B * S
    nq, nk = S // tq, S // tk
    assert (nq * nk) % 2 == 0 and tk // FLASH_QK_ROWS > FLASH_PV_AFTER
    return pl.pallas_call(
        functools.partial(_mla_flash_kernel, tq=tq, tk=tk, nk=nk, nq=nq),
        grid=(B, MLA_HEADS),
        in_specs=[
            pl.BlockSpec((QK_PAD, S), lambda b, h: (h, b)),
            pl.BlockSpec((S, QK_PAD), lambda b, h: (b, h)),
            pl.BlockSpec((VT_ROWS, S), lambda b, h: (h, b)),
        ],
        out_specs=pl.BlockSpec((S, V_HEAD), lambda b, h: (b, h)),
        out_shape=jax.ShapeDtypeStruct((T, MLA_HEADS * V_HEAD), BF16),
        scratch_shapes=[pltpu.VMEM((tk, tq), BF16), pltpu.VMEM((tk, tq), BF16), pltpu.VMEM((VT_ROWS, tq), F32)],
        compiler_params=_cparams(2),
        name="mla_flash",
    )(qt, k, vt)


def _dil_kernel(q_ref, k_ref, v_ref, bias_ref, o_ref, lse_ref, *, L, ta):
    scale = DIL_HEAD_DIM ** -0.5
    lane = lax.broadcasted_iota(jnp.int32, (DIL_QB, 128), 1)
    base = pl.program_id(2) * ta
    for t in range(ta // DIL_QB):
        a0 = base + t * DIL_QB
        k0 = jnp.clip(a0 - DIL_SIDE, 0, L - DIL_KB)
        var = (a0 - k0) // DIL_SIDE
        k0 = pl.multiple_of(k0, DIL_SIDE)
        rows = slice(t * DIL_QB, (t + 1) * DIL_QB)
        lses = []
        for h in range(DIL_HEADS_PER_GROUP):
            cs = slice(h * DIL_HEAD_DIM, (h + 1) * DIL_HEAD_DIM)
            qh = q_ref[0, 0, rows, cs]
            kh = k_ref[0, 0, pl.ds(k0, DIL_KB), cs]
            vh = v_ref[0, 0, pl.ds(k0, DIL_KB), cs]
            s = _dot_nt(qh, kh) * scale + bias_ref[var, h]
            m = jnp.max(s, axis=-1, keepdims=True)
            p = jnp.exp(s - m)
            l = jnp.sum(p, axis=-1, keepdims=True)
            o_ref[0, 0, rows, cs] = _dot(p.astype(BF16), vh) / l
            lses.append(m + jnp.log(l))
        packed = jnp.where(lane < 32, lses[0], jnp.where(lane < 64, lses[1], jnp.where(lane < 96, lses[2], lses[3])))
        lse_ref[0, 0, rows, :] = packed


def _dilated_group(q, k, v, bias, *, qcol, kcol, vcol, name):
    B, r, L, _ = q.shape
    ta = min(L, 512)
    return pl.pallas_call(
        functools.partial(_dil_kernel, L=L, ta=ta),
        grid=(B, r, L // ta),
        in_specs=[
            pl.BlockSpec((1, 1, ta, DIL_W), lambda b, c, a: (b, c, a, qcol)),
            pl.BlockSpec((1, 1, L, DIL_W), lambda b, c, a: (b, c, 0, kcol)),
            pl.BlockSpec((1, 1, L, DIL_W), lambda b, c, a: (b, c, 0, vcol)),
            pl.BlockSpec(bias.shape, lambda b, c, a: (0, 0, 0, 0)),
        ],
        out_specs=[
            pl.BlockSpec((1, 1, ta, DIL_W), lambda b, c, a: (b, c, a, 0)),
            pl.BlockSpec((1, 1, ta, 128), lambda b, c, a: (b, c, a, 0)),
        ],
        out_shape=[
            jax.ShapeDtypeStruct((B, r, L, DIL_W), F32),
            jax.ShapeDtypeStruct((B, r, L, 128), F32),
        ],
        compiler_params=_cparams(3),
        name=name,
    )(q, k, v, bias)


def _dilated(proj, bias_tabs, *, B, S):
    T = B * S
    ods, lss = [], []
    for g, (_, r) in enumerate(DIL_PAIRS):
        L = S // r
        if r == 1:
            p4 = proj.reshape(B, 1, S, PROJ_W)
            od, ls = _dilated_group(p4, p4, p4, bias_tabs[g], qcol=_PROJ_DQ // DIL_W + g, kcol=_PROJ_DK // DIL_W + g,
                                    vcol=_PROJ_DV // DIL_W + g, name=f"dilated_g{g}")
        else:
            def stream(c0):
                x = lax.slice_in_dim(proj, c0 + g * DIL_W, c0 + (g + 1) * DIL_W, axis=1)
                return x.reshape(B, L, r, DIL_W).transpose(0, 2, 1, 3)
            od, ls = _dilated_group(stream(_PROJ_DQ), stream(_PROJ_DK), stream(_PROJ_DV), bias_tabs[g],
                                    qcol=0, kcol=0, vcol=0, name=f"dilated_g{g}")
            od = od.transpose(0, 2, 1, 3)
            ls = ls.transpose(0, 2, 1, 3)
        ods.append(od.reshape(T, DIL_W))
        lss.append(ls.reshape(T, 128))
    return ods, lss


def _t5_bucket(rel):
    nb = NUM_BUCKETS // 2
    ret = (rel > 0).astype(np.int32) * nb
    n = np.abs(rel)
    max_exact = nb // 2
    large = max_exact + (np.log(np.maximum(n, 1) / max_exact) / np.log(MAX_DISTANCE / max_exact)
                         * (nb - max_exact)).astype(np.int32)
    large = np.minimum(large, nb - 1)
    return (ret + np.where(n < max_exact, n, large)).astype(np.int32)


def _dil_bias_tables(rel_bias):
    qq = np.arange(DIL_QB)[:, None]
    kk = np.arange(DIL_KB)[None, :]
    ext = jnp.concatenate([rel_bias.astype(F32), jnp.full((1, rel_bias.shape[1]), NEG_INF, F32)], axis=0)
    tables = []
    for g, (_, r) in enumerate(DIL_PAIRS):
        idx = []
        for var in range(3):
            j = kk - qq - var * DIL_SIDE
            bucket = _t5_bucket(j * r)
            idx.append(np.where(np.abs(j) <= DIL_SIDE, bucket, NUM_BUCKETS))
        onehot = (np.stack(idx)[..., None] == np.arange(NUM_BUCKETS + 1)).astype(np.float32)
        heads = ext[:, g * DIL_HEADS_PER_GROUP:(g + 1) * DIL_HEADS_PER_GROUP]
        tables.append(jnp.einsum("vqkn,nh->vhqk", onehot, heads, precision=lax.Precision.HIGHEST))
    return tables


def _mem_kv_kernel(m_ref, g_ref, w_ref, o_ref):
    o_ref[...] = _dot(_rms(m_ref[...], g_ref[...]).astype(BF16), w_ref[...]).astype(BF16)


def _mem_kv(mem2, g, w):
    R, D = mem2.shape
    N = w.shape[1]
    return pl.pallas_call(
        _mem_kv_kernel,
        grid=(R // N_MEM,),
        in_specs=[
            pl.BlockSpec((N_MEM, D), lambda i: (i, 0)),
            pl.BlockSpec((1, D), lambda i: (0, 0)),
            pl.BlockSpec((D, N), lambda i: (0, 0)),
        ],
        out_specs=pl.BlockSpec((N_MEM, N), lambda i: (i, 0)),
        out_shape=jax.ShapeDtypeStruct((R, N), BF16),
        compiler_params=_cparams(1),
        name="mem_kv",
    )(mem2, g, w)


def _mem_attn_kernel(q_ref, kv_ref, o_ref):
    scale = X_HEAD_DIM ** -0.5
    hv = X_HEADS * X_HEAD_DIM
    for h in range(X_HEADS):
        cs = slice(h * X_HEAD_DIM, (h + 1) * X_HEAD_DIM)
        s = _dot_nt(q_ref[:, cs], kv_ref[:, cs]) * scale
        m = jnp.max(s, axis=-1, keepdims=True)
        p = jnp.exp(s - m)
        l = jnp.sum(p, axis=-1, keepdims=True)
        o = _dot(p.astype(BF16), kv_ref[:, hv + h * X_HEAD_DIM:hv + (h + 1) * X_HEAD_DIM]) / l
        o_ref[:, cs] = o.astype(BF16)


def _mem_attn(proj, kvm, *, S, tm=512):
    T = proj.shape[0]
    W = X_HEADS * X_HEAD_DIM
    nsb = S // tm
    return pl.pallas_call(
        _mem_attn_kernel,
        grid=(T // tm,),
        in_specs=[
            pl.BlockSpec((tm, W), lambda i: (i, _PROJ_XQ // W)),
            pl.BlockSpec((N_MEM, 2 * W), lambda i: (i // nsb, 0)),
        ],
        out_specs=pl.BlockSpec((tm, W), lambda i: (i, 0)),
        out_shape=jax.ShapeDtypeStruct((T, W), BF16),
        compiler_params=_cparams(1),
        name="mem_attn",
    )(proj, kvm)


def _merge_kernel(gate_ref, omla_ref, od0_ref, od1_ref, od2_ref, ls0_ref, ls1_ref, ls2_ref, omem_ref,
                  wmla_ref, wdil_ref, wmem_ref, o_ref):
    ls = [ls0_ref[...], ls1_ref[...], ls2_ref[...]]
    mx = jnp.maximum(jnp.maximum(ls[0], ls[1]), ls[2])
    es = [jnp.exp(x - mx) for x in ls]
    inv = 1.0 / (es[0] + es[1] + es[2])
    od = [od0_ref, od1_ref, od2_ref]
    parts = []
    for h in range(DIL_HEADS_PER_GROUP):
        cs = slice(h * DIL_HEAD_DIM, (h + 1) * DIL_HEAD_DIM)
        acc = None
        for gi in range(DIL_GROUPS):
            a = (es[gi] * inv)[:, h * 32:h * 32 + 1]
            term = a * od[gi][:, cs]
            acc = term if acc is None else acc + term
        parts.append(acc.astype(BF16))
    o_dil = jnp.concatenate(parts, axis=-1)

    def gate(b):
        return jax.nn.sigmoid(gate_ref[:, b * D_MODEL:(b + 1) * D_MODEL].astype(F32))

    merged = gate(0) * _dot(omla_ref[...], wmla_ref[...])
    merged += gate(1) * _dot(o_dil, wdil_ref[...])
    merged += gate(2) * _dot(omem_ref[...], wmem_ref[...])
    o_ref[...] = merged.astype(BF16)


def _merge(proj, o_mla, ods, lss, o_mem, wmla, wdil, wmem, *, tm=512):
    T = proj.shape[0]
    row = lambda i: (i, 0)
    const = lambda i: (0, 0)
    wspec = lambda w: pl.BlockSpec(w.shape, const, pipeline_mode=pl.Buffered(1))
    return pl.pallas_call(
        _merge_kernel,
        grid=(T // tm,),
        in_specs=[
            pl.BlockSpec((tm, N_BRANCH * D_MODEL), row),
            pl.BlockSpec((tm, o_mla.shape[1]), row),
            pl.BlockSpec((tm, DIL_W), row), pl.BlockSpec((tm, DIL_W), row), pl.BlockSpec((tm, DIL_W), row),
            pl.BlockSpec((tm, 128), row), pl.BlockSpec((tm, 128), row), pl.BlockSpec((tm, 128), row),
            pl.BlockSpec((tm, o_mem.shape[1]), row),
            wspec(wmla), wspec(wdil), wspec(wmem),
        ],
        out_specs=pl.BlockSpec((tm, D_MODEL), row),
        out_shape=jax.ShapeDtypeStruct((T, D_MODEL), BF16),
        compiler_params=_cparams(1),
        name="merge",
    )(proj, o_mla, *ods, *lss, o_mem, wmla, wdil, wmem)


def _out_mlp_kernel(x_ref, mg_ref, wout_ref, gm_ref, wup_ref, wdn_ref, gf_ref, o_ref, h_ref):
    j = pl.program_id(1)

    @pl.when(j == 0)
    def _():
        x1 = x_ref[...] + _dot(mg_ref[...], wout_ref[...])
        o_ref[...] = x1
        h_ref[...] = _rms(x1, gm_ref[...]).astype(BF16)

    u = jnp.maximum(_dot(h_ref[...], wup_ref[...]), 0.0)
    o_ref[...] += _dot((u * u).astype(BF16), wdn_ref[...])

    @pl.when(j == pl.num_programs(1) - 1)
    def _():
        o_ref[...] = _rms(o_ref[...], gf_ref[...])


def _out_mlp(x2, merged, wout, gm, wup, wdn, gf, *, tm=512, tf=1024):
    T, D = x2.shape
    row = lambda i, j: (i, 0)
    const = lambda i, j: (0, 0)
    return pl.pallas_call(
        _out_mlp_kernel,
        grid=(T // tm, D_FF // tf),
        in_specs=[
            pl.BlockSpec((tm, D), row),
            pl.BlockSpec((tm, D), row),
            pl.BlockSpec((D, D), const, pipeline_mode=pl.Buffered(1)),
            pl.BlockSpec((1, D), const),
            pl.BlockSpec((D, tf), lambda i, j: (0, j)),
            pl.BlockSpec((tf, D), lambda i, j: (j, 0)),
            pl.BlockSpec((1, D), const),
        ],
        out_specs=pl.BlockSpec((tm, D), row),
        out_shape=jax.ShapeDtypeStruct((T, D), F32),
        scratch_shapes=[pltpu.VMEM((tm, D), BF16)],
        compiler_params=_cparams(2),
        name="out_mlp",
    )(x2, merged, wout, gm, wup, wdn, gf)


def _rot_half_cols(w):
    half = QK_ROPE // 2
    return jnp.concatenate([-w[..., half:], w[..., :half]], axis=-1)


def _prep_weights(w_in, w_uq, w_ukv, w_mem_kv, w_b_mla, w_b_dil, w_b_mem, w_out, w_up, w_down):
    D = w_in.shape[0]
    o = 0
    cq = w_in[:, o:o + Q_LORA]; o += Q_LORA
    ckv = w_in[:, o:o + KV_LORA]; o += KV_LORA
    kr = w_in[:, o:o + QK_ROPE]; o += QK_ROPE
    dil = w_in[:, o:o + 3 * DIL_GROUPS * DIL_W]; o += 3 * DIL_GROUPS * DIL_W
    xq = w_in[:, o:o + X_HEADS * X_HEAD_DIM]; o += X_HEADS * X_HEAD_DIM
    gate = w_in[:, o:o + N_BRANCH * D_MODEL]
    z64 = jnp.zeros((D, 64), w_in.dtype)
    w_in_p = jnp.concatenate(
        [gate, xq, dil, cq, ckv, kr, z64, _rot_half_cols(kr), z64, jnp.zeros((D, 256), w_in.dtype)], axis=1).astype(BF16)

    wkv = w_ukv.reshape(KV_LORA, MLA_HEADS, QK_NOPE + V_HEAD)
    wk = wkv[..., :QK_NOPE].reshape(KV_LORA, MLA_HEADS * QK_NOPE).astype(BF16)
    wvt = wkv[..., QK_NOPE:].reshape(KV_LORA, MLA_HEADS * V_HEAD).T.astype(BF16)
    return dict(w_in=w_in_p, wqt=w_uq.T.astype(BF16), wk=wk, wvt=wvt, wmkv=w_mem_kv.astype(BF16),
                wmla=w_b_mla.astype(BF16), wdil=w_b_dil.astype(BF16), wmem=w_b_mem.astype(BF16),
                wout=w_out.astype(BF16), wup=w_up.astype(BF16), wdn=w_down.astype(BF16))


def _rope_tables(S):
    half = QK_ROPE // 2
    inv = 1.0 / (ROPE_THETA ** (jnp.arange(half, dtype=F32) / half))
    ang = jnp.arange(S).astype(F32)[:, None] * inv[None, :]
    z = jnp.zeros((S, 64), F32)
    cos, sin = jnp.cos(ang), jnp.sin(ang)
    return jnp.concatenate([cos, cos, z], axis=1), jnp.concatenate([sin, sin, z], axis=1), cos.T, sin.T


def _trunk(x, mem, W, bias_tabs, g_attn, g_qn, g_kvn, g_mem, g_mlp, g_final):
    B, S, D = x.shape
    T = B * S
    x2 = x.reshape(T, D)
    row = lambda g: g.reshape(1, -1).astype(F32)

    proj = _in_proj(x2, row(g_attn), W["w_in"])

    q_scale = (QK_NOPE + QK_ROPE) ** -0.5 * math.log2(math.e)
    qt, k, vt = _mla_proj(proj, _rope_tables(S), row(g_qn), row(g_kvn), W["wqt"], W["wk"], W["wvt"], S=S,
                          q_scale=q_scale)
    o_mla = _mla_flash(qt, k, vt, B=B, S=S)

    ods, lss = _dilated(proj, bias_tabs, B=B, S=S)

    kvm = _mem_kv(mem.reshape(B * N_MEM, D), row(g_mem), W["wmkv"])
    o_mem = _mem_attn(proj, kvm, S=S)

    merged = _merge(proj, o_mla, ods, lss, o_mem, W["wmla"], W["wdil"], W["wmem"])
    y = _out_mlp(x2, merged, W["wout"], row(g_mlp), W["wup"], W["wdn"], row(g_final))
    return y.reshape(B, S, D)


def kernel(x_prompt, x_sample, mem_prompt, mem_sample, rel_bias, g_attn, w_in, g_q_norm, w_uq, g_kv_norm, w_ukv,
           g_mem, w_mem_kv, w_b_mla, w_b_dil, w_b_mem, w_out, g_mlp, w_up, w_down, g_final):
    assert w_in.shape[0] == 1, "single layer"
    W = _prep_weights(w_in[0], w_uq[0], w_ukv[0], w_mem_kv[0], w_b_mla[0], w_b_dil[0], w_b_mem[0], w_out[0],
                      w_up[0], w_down[0])
    bias_tabs = _dil_bias_tables(rel_bias)
    args = (W, bias_tabs, g_attn[0], g_q_norm[0], g_kv_norm[0], g_mem[0], g_mlp[0], g_final)
    return (_trunk(x_prompt, mem_prompt, *args), _trunk(x_sample, mem_sample, *args))
```

```python
import functools
import math

import numpy as np
import jax
import jax.numpy as jnp
from jax import lax
from jax.experimental import pallas as pl
from jax.experimental.pallas import tpu as pltpu

F32 = jnp.float32
BF16 = jnp.bfloat16

D_MODEL = 2048
N_MEM = 256
MLA_HEADS = 12
Q_LORA = 512
KV_LORA = 512
QK_NOPE = 128
QK_ROPE = 64
V_HEAD = 128
ROPE_THETA = 10000.0
DIL_PAIRS = ((128, 1), (512, 4), (2048, 16))
DIL_GROUPS = 3
DIL_HEADS_PER_GROUP = 4
DIL_HEAD_DIM = 128
X_HEADS = 4
X_HEAD_DIM = 256
NUM_BUCKETS = 32
MAX_DISTANCE = 1024
D_FF = 4 * D_MODEL
N_BRANCH = 3
EPS = 1e-6
NEG_INF = -1e30

DIL_W = DIL_HEADS_PER_GROUP * DIL_HEAD_DIM
DIL_SIDE = 64
DIL_QB = 128
DIL_KB = DIL_QB + 2 * DIL_SIDE
QK_PAD = 256
VT_ROWS = V_HEAD + 16

_PROJ_GATE = 0
_PROJ_XQ = _PROJ_GATE + N_BRANCH * D_MODEL
_PROJ_DIL = _PROJ_XQ + X_HEADS * X_HEAD_DIM
_PROJ_CQ = _PROJ_DIL + 3 * DIL_GROUPS * DIL_W
_PROJ_CKV = _PROJ_CQ + Q_LORA
_PROJ_KR = _PROJ_CKV + KV_LORA
PROJ_W = _PROJ_KR + 512

VMEM_LIMIT = 52 * 1024 * 1024


def _cparams(n_axes):
    return pltpu.CompilerParams(dimension_semantics=("arbitrary",) * n_axes, vmem_limit_bytes=VMEM_LIMIT)


def _rms(x32, g32):
    return x32 * lax.rsqrt(jnp.mean(x32 * x32, axis=-1, keepdims=True) + EPS) * g32


def _dot(a, b):
    return jnp.dot(a, b, preferred_element_type=F32)


def _dot_nt(a, b):
    return lax.dot_general(a, b, (((1,), (1,)), ((), ())), preferred_element_type=F32)


def _in_proj_kernel(x_ref, g_ref, w_ref, o_ref, h_ref):
    @pl.when(pl.program_id(1) == 0)
    def _():
        h_ref[...] = _rms(x_ref[...], g_ref[...]).astype(BF16)

    o_ref[...] = _dot(h_ref[...], w_ref[...]).astype(BF16)


def _in_proj(x2, g, w, *, tm=1024, tn=1024):
    T, D = x2.shape
    return pl.pallas_call(
        _in_proj_kernel,
        grid=(T // tm, PROJ_W // tn),
        in_specs=[
            pl.BlockSpec((tm, D), lambda i, j: (i, 0)),
            pl.BlockSpec((1, D), lambda i, j: (0, 0)),
            pl.BlockSpec((D, tn), lambda i, j: (0, j)),
        ],
        out_specs=pl.BlockSpec((tm, tn), lambda i, j: (i, j)),
        out_shape=jax.ShapeDtypeStruct((T, PROJ_W), BF16),
        scratch_shapes=[pltpu.VMEM((tm, D), BF16)],
        compiler_params=_cparams(2),
        name="in_proj",
    )(x2, g, w)


def _mla_proj_kernel(cq_ref, ckv_ref, kr_ref, cos_ref, sin_ref, cost_ref, sint_ref, gq_ref, gkv_ref,
                     wqt_ref, wk_ref, wvt_ref, qt_ref, k_ref, vt_ref, *, q_scale):
    half = QK_ROPE // 2
    hd = QK_NOPE + QK_ROPE
    nq_t = _rms(cq_ref[...].astype(F32), gq_ref[...]).T.astype(BF16)
    q_t = _dot(wqt_ref[...], nq_t)
    cos_t = cost_ref[...]
    sin_t = sint_ref[...]
    zeros = jnp.zeros((QK_PAD - hd, q_t.shape[1]), BF16)
    for h in range(MLA_HEADS):
        r0, o0 = h * hd, h * QK_PAD
        x1 = q_t[r0 + QK_NOPE:r0 + QK_NOPE + half]
        x2 = q_t[r0 + QK_NOPE + half:r0 + hd]
        qt_ref[o0:o0 + QK_NOPE, :] = (q_t[r0:r0 + QK_NOPE] * q_scale).astype(BF16)
        qt_ref[o0 + QK_NOPE:o0 + QK_NOPE + half, :] = ((x1 * cos_t - x2 * sin_t) * q_scale).astype(BF16)
        qt_ref[o0 + QK_NOPE + half:o0 + hd, :] = ((x2 * cos_t + x1 * sin_t) * q_scale).astype(BF16)
        qt_ref[o0 + hd:o0 + QK_PAD, :] = zeros

    nkv = _rms(ckv_ref[...].astype(F32), gkv_ref[...])
    v_t = _dot(wvt_ref[...], nkv.T.astype(BF16)).astype(BF16)
    extra = jnp.where(lax.broadcasted_iota(jnp.int32, (VT_ROWS - V_HEAD, v_t.shape[1]), 0) == 0, 1.0, 0.0).astype(BF16)
    for h in range(MLA_HEADS):
        vt_ref[h * VT_ROWS:h * VT_ROWS + V_HEAD, :] = v_t[h * V_HEAD:(h + 1) * V_HEAD]
        vt_ref[h * VT_ROWS + V_HEAD:(h + 1) * VT_ROWS, :] = extra
    kn = _dot(nkv.astype(BF16), wk_ref[...])
    kr = kr_ref[...].astype(F32)
    kpe = (kr[:, 0:128] * cos_ref[...] + kr[:, 128:256] * sin_ref[...]).astype(BF16)
    for h in range(MLA_HEADS):
        o0 = h * QK_PAD
        k_ref[:, o0:o0 + QK_NOPE] = kn[:, h * QK_NOPE:(h + 1) * QK_NOPE].astype(BF16)
        k_ref[:, o0 + QK_NOPE:o0 + QK_PAD] = kpe


def _mla_proj(proj, rope, gq, gkv, wqt, wk, wvt, *, S, q_scale, tm=512):
    T = proj.shape[0]
    nsb = S // tm
    HQ = MLA_HEADS * QK_PAD
    HV = MLA_HEADS * VT_ROWS
    cos_k, sin_k, cos_t, sin_t = rope
    const = lambda i: (0, 0)
    wspec = lambda w: pl.BlockSpec(w.shape, const, pipeline_mode=pl.Buffered(1))
    return pl.pallas_call(
        functools.partial(_mla_proj_kernel, q_scale=q_scale),
        grid=(T // tm,),
        in_specs=[
            pl.BlockSpec((tm, Q_LORA), lambda i: (i, _PROJ_CQ // 512)),
            pl.BlockSpec((tm, KV_LORA), lambda i: (i, _PROJ_CKV // 512)),
            pl.BlockSpec((tm, 512), lambda i: (i, _PROJ_KR // 512)),
            pl.BlockSpec((tm, 128), lambda i: (i % nsb, 0)),
            pl.BlockSpec((tm, 128), lambda i: (i % nsb, 0)),
            pl.BlockSpec((QK_ROPE // 2, tm), lambda i: (0, i % nsb)),
            pl.BlockSpec((QK_ROPE // 2, tm), lambda i: (0, i % nsb)),
            pl.BlockSpec((1, Q_LORA), const),
            pl.BlockSpec((1, KV_LORA), const),
            wspec(wqt), wspec(wk), wspec(wvt),
        ],
        out_specs=[
            pl.BlockSpec((HQ, tm), lambda i: (0, i)),
            pl.BlockSpec((tm, HQ), lambda i: (i, 0)),
            pl.BlockSpec((HV, tm), lambda i: (0, i)),
        ],
        out_shape=[
            jax.ShapeDtypeStruct((HQ, T), BF16),
            jax.ShapeDtypeStruct((T, HQ), BF16),
            jax.ShapeDtypeStruct((HV, T), BF16),
        ],
        compiler_params=_cparams(1),
        name="mla_proj",
    )(proj, proj, proj, cos_k, sin_k, cos_t, sin_t, gq, gkv, wqt, wk, wvt)


FLASH_SLAB = 32
FLASH_QK_ROWS = 256


FLASH_WINDOW = 100.0
FLASH_PV_AFTER = 4
FLASH_EXACT_ROWS = 512


def _zero_like_bits(v):
    u = pltpu.bitcast(v, jnp.uint32)
    return pltpu.bitcast(lax.shift_right_logical(lax.shift_right_logical(u, jnp.uint32(16)), jnp.uint32(16)), F32)


def _mla_flash_kernel(qt_ref, k_ref, vt_ref, o_ref, p0_ref, p1_ref, acc_ref, *, tq, tk, nk, nq):
    p_refs = (p0_ref, p1_ref)
    nc = nq * nk

    def chunk_pos(c):
        qi = c // nk
        return qi, c == qi * nk, pl.multiple_of((c - qi * nk) * tk, tk), pl.multiple_of(qi * tq, tq)

    def pv(c, slot):
        _, first, r0, _ = chunk_pos(c)
        acc = jnp.where(first, 0.0, acc_ref[...]) + _dot(vt_ref[:, pl.ds(r0, tk)], p_refs[slot][...])
        acc_ref[...] = acc
        return acc[0:1]

    def qk_exp(c, slot, st, pv_args=None):
        ref, rmax, done_ref, done_rmax = st
        _, first, r0, c0 = chunk_pos(c)
        done_ref = jnp.where(first, ref, done_ref)
        done_rmax = jnp.where(first, rmax, done_rmax)
        ref = jnp.where(first, rmax, ref)
        rmax = jnp.where(first, -jnp.inf, rmax)
        mx = None
        shift = ref
        for n, q0 in enumerate(range(0, tk, FLASH_QK_ROWS)):
            if n == FLASH_PV_AFTER and pv_args is not None:
                shift = ref + _zero_like_bits(pv(*pv_args))
            s = _dot(k_ref[pl.ds(r0 + q0, FLASH_QK_ROWS), :], qt_ref[:, pl.ds(c0, tq)])
            p_refs[slot][q0:q0 + FLASH_QK_ROWS, :] = jnp.exp2(s - shift).astype(BF16)
            part = jnp.max(s.reshape(FLASH_QK_ROWS // FLASH_SLAB, FLASH_SLAB, tq), axis=0)
            mx = part if mx is None else jnp.maximum(mx, part)
        rmax = jnp.maximum(rmax, jnp.max(mx, axis=0, keepdims=True))
        return ref, rmax, done_ref, done_rmax

    def exact_block(qi):
        qt = qt_ref[:, pl.ds(pl.multiple_of(qi * tq, tq), tq)]

        def step(j, carry):
            m, acc = carry
            r0 = pl.multiple_of(j * FLASH_EXACT_ROWS, FLASH_EXACT_ROWS)
            s = _dot(k_ref[pl.ds(r0, FLASH_EXACT_ROWS), :], qt)
            m_new = jnp.maximum(m, jnp.max(s, axis=0, keepdims=True))
            p = jnp.exp2(s - m_new).astype(BF16)
            return m_new, acc * jnp.exp2(m - m_new) + _dot(vt_ref[:, pl.ds(r0, FLASH_EXACT_ROWS)], p)

        init = (jnp.full((1, tq), -jnp.inf, F32), jnp.zeros((VT_ROWS, tq), F32))
        _, acc = lax.fori_loop(0, nk * tk // FLASH_EXACT_ROWS, step, init)
        acc_ref[...] = acc

    def finalize(c, ref_b, rmax_b):
        qi = c // nk
        in_window = jnp.max(jnp.abs(rmax_b - ref_b)) <= FLASH_WINDOW

        @pl.when(jnp.logical_not(in_window))
        def _():
            exact_block(qi)

        acc = acc_ref[...]
        o = acc[:V_HEAD] * (1.0 / acc[V_HEAD:V_HEAD + 1])
        o_ref[pl.ds(pl.multiple_of(qi * tq, tq), tq), :] = o.T.astype(BF16)

    mx = None
    for q0 in range(0, tk, FLASH_QK_ROWS):
        s = _dot(k_ref[q0:q0 + FLASH_QK_ROWS, :], qt_ref[:, 0:tq])
        part = jnp.max(s.reshape(FLASH_QK_ROWS // FLASH_SLAB, FLASH_SLAB, tq), axis=0)
        mx = part if mx is None else jnp.maximum(mx, part)
    first_max = jnp.max(mx, axis=0, keepdims=True)
    ninf = jnp.full((1, tq), -jnp.inf, F32)
    st = qk_exp(0, 0, (first_max, first_max, ninf, ninf))

    def chunks_of_block(i, st, n):
        for j in range(n):
            c = i * nk + j
            st = qk_exp(c + 1, (j + 1) % 2, st, (c, j % 2))
        return st

    def block(i, st):
        st = chunks_of_block(i, st, nk)
        finalize(i * nk + nk - 1, st[2], st[3])
        return st

    st = lax.fori_loop(0, nq - 1, block, st)
    st = chunks_of_block(nq - 1, st, nk - 1)
    pv(nc - 1, (nk - 1) % 2)
    finalize(nc - 1, st[0], st[1])


def _mla_flash(qt, k, vt, *, B, S, tq=512, tk=2048):
    T = B * S
    nq, nk = S // tq, S // tk
    assert nk % 2 == 0 and tk // FLASH_QK_ROWS > FLASH_PV_AFTER
    return pl.pallas_call(
        functools.partial(_mla_flash_kernel, tq=tq, tk=tk, nk=nk, nq=nq),
        grid=(B, MLA_HEADS),
        in_specs=[
            pl.BlockSpec((QK_PAD, S), lambda b, h: (h, b)),
            pl.BlockSpec((S, QK_PAD), lambda b, h: (b, h)),
            pl.BlockSpec((VT_ROWS, S), lambda b, h: (h, b)),
        ],
        out_specs=pl.BlockSpec((S, V_HEAD), lambda b, h: (b, h)),
        out_shape=jax.ShapeDtypeStruct((T, MLA_HEADS * V_HEAD), BF16),
        scratch_shapes=[pltpu.VMEM((tk, tq), BF16), pltpu.VMEM((tk, tq), BF16), pltpu.VMEM((VT_ROWS, tq), F32)],
        compiler_params=_cparams(2),
        name="mla_flash",
    )(qt, k, vt)


def _dil_kernel(q_ref, k_ref, v_ref, bias_ref, o_ref, lse_ref, *, L, ta):
    scale = DIL_HEAD_DIM ** -0.5
    lane = lax.broadcasted_iota(jnp.int32, (DIL_QB, 128), 1)
    base = pl.program_id(2) * ta
    for t in range(ta // DIL_QB):
        a0 = base + t * DIL_QB
        k0 = jnp.clip(a0 - DIL_SIDE, 0, L - DIL_KB)
        var = (a0 - k0) // DIL_SIDE
        k0 = pl.multiple_of(k0, DIL_SIDE)
        rows = slice(t * DIL_QB, (t + 1) * DIL_QB)
        lses = []
        for h in range(DIL_HEADS_PER_GROUP):
            cs = slice(h * DIL_HEAD_DIM, (h + 1) * DIL_HEAD_DIM)
            qh = q_ref[0, 0, rows, cs]
            kh = k_ref[0, 0, pl.ds(k0, DIL_KB), cs]
            vh = v_ref[0, 0, pl.ds(k0, DIL_KB), cs]
            s = _dot_nt(qh, kh) * scale + bias_ref[var, h]
            m = jnp.max(s, axis=-1, keepdims=True)
            p = jnp.exp(s - m)
            l = jnp.sum(p, axis=-1, keepdims=True)
            o_ref[0, 0, rows, cs] = (_dot(p.astype(BF16), vh) / l).astype(BF16)
            lses.append(m + jnp.log(l))
        packed = jnp.where(lane < 32, lses[0], jnp.where(lane < 64, lses[1], jnp.where(lane < 96, lses[2], lses[3])))
        lse_ref[0, 0, rows, :] = packed


def _dilated_group(q, k, v, bias, *, qcol, kcol, vcol, name):
    B, r, L, _ = q.shape
    ta = min(L, 512)
    return pl.pallas_call(
        functools.partial(_dil_kernel, L=L, ta=ta),
        grid=(B, r, L // ta),
        in_specs=[
            pl.BlockSpec((1, 1, ta, DIL_W), lambda b, c, a: (b, c, a, qcol)),
            pl.BlockSpec((1, 1, L, DIL_W), lambda b, c, a: (b, c, 0, kcol)),
            pl.BlockSpec((1, 1, L, DIL_W), lambda b, c, a: (b, c, 0, vcol)),
            pl.BlockSpec(bias.shape, lambda b, c, a: (0, 0, 0, 0)),
        ],
        out_specs=[
            pl.BlockSpec((1, 1, ta, DIL_W), lambda b, c, a: (b, c, a, 0)),
            pl.BlockSpec((1, 1, ta, 128), lambda b, c, a: (b, c, a, 0)),
        ],
        out_shape=[
            jax.ShapeDtypeStruct((B, r, L, DIL_W), BF16),
            jax.ShapeDtypeStruct((B, r, L, 128), F32),
        ],
        compiler_params=_cparams(3),
        name=name,
    )(q, k, v, bias)


def _dilated(proj, bias_tabs, *, B, S):
    T = B * S
    ods, lss = [], []
    for g, (_, r) in enumerate(DIL_PAIRS):
        L = S // r
        c0 = _PROJ_DIL + g * 3 * DIL_W
        if r == 1:
            x, col = proj.reshape(B, 1, S, PROJ_W), c0 // DIL_W
        else:
            x = lax.slice_in_dim(proj, c0, c0 + 3 * DIL_W, axis=1).reshape(B, L, r, 3 * DIL_W).transpose(0, 2, 1, 3)
            col = 0
        od, ls = _dilated_group(x, x, x, bias_tabs[g], qcol=col, kcol=col + 1, vcol=col + 2, name=f"dilated_g{g}")
        if r != 1:
            od = od.transpose(0, 2, 1, 3)
            ls = ls.transpose(0, 2, 1, 3)
        ods.append(od.reshape(T, DIL_W))
        lss.append(ls.reshape(T, 128))
    return ods, lss


def _t5_bucket(rel):
    nb = NUM_BUCKETS // 2
    ret = (rel > 0).astype(np.int32) * nb
    n = np.abs(rel)
    max_exact = nb // 2
    large = max_exact + (np.log(np.maximum(n, 1) / max_exact) / np.log(MAX_DISTANCE / max_exact)
                         * (nb - max_exact)).astype(np.int32)
    large = np.minimum(large, nb - 1)
    return (ret + np.where(n < max_exact, n, large)).astype(np.int32)


def _dil_bias_tables(rel_bias):
    qq = np.arange(DIL_QB)[:, None]
    kk = np.arange(DIL_KB)[None, :]
    ext = jnp.concatenate([rel_bias.astype(F32), jnp.full((1, rel_bias.shape[1]), NEG_INF, F32)], axis=0)
    tables = []
    for g, (_, r) in enumerate(DIL_PAIRS):
        idx = []
        for var in range(3):
            j = kk - qq - var * DIL_SIDE
            bucket = _t5_bucket(j * r)
            idx.append(np.where(np.abs(j) <= DIL_SIDE, bucket, NUM_BUCKETS))
        onehot = (np.stack(idx)[..., None] == np.arange(NUM_BUCKETS + 1)).astype(np.float32)
        heads = ext[:, g * DIL_HEADS_PER_GROUP:(g + 1) * DIL_HEADS_PER_GROUP]
        tables.append(jnp.einsum("vqkn,nh->vhqk", onehot, heads, precision=lax.Precision.HIGHEST))
    return tables


def _mem_kv_kernel(m_ref, g_ref, w_ref, o_ref):
    o_ref[...] = _dot(_rms(m_ref[...], g_ref[...]).astype(BF16), w_ref[...]).astype(BF16)


def _mem_kv(mem2, g, w):
    R, D = mem2.shape
    N = w.shape[1]
    return pl.pallas_call(
        _mem_kv_kernel,
        grid=(R // N_MEM,),
        in_specs=[
            pl.BlockSpec((N_MEM, D), lambda i: (i, 0)),
            pl.BlockSpec((1, D), lambda i: (0, 0)),
            pl.BlockSpec((D, N), lambda i: (0, 0)),
        ],
        out_specs=pl.BlockSpec((N_MEM, N), lambda i: (i, 0)),
        out_shape=jax.ShapeDtypeStruct((R, N), BF16),
        compiler_params=_cparams(1),
        name="mem_kv",
    )(mem2, g, w)


def _mem_attn_kernel(q_ref, kv_ref, o_ref):
    scale = X_HEAD_DIM ** -0.5
    hv = X_HEADS * X_HEAD_DIM
    for h in range(X_HEADS):
        cs = slice(h * X_HEAD_DIM, (h + 1) * X_HEAD_DIM)
        s = _dot_nt(q_ref[:, cs], kv_ref[:, cs]) * scale
        m = jnp.max(s, axis=-1, keepdims=True)
        p = jnp.exp(s - m)
        l = jnp.sum(p, axis=-1, keepdims=True)
        o = _dot(p.astype(BF16), kv_ref[:, hv + h * X_HEAD_DIM:hv + (h + 1) * X_HEAD_DIM]) / l
        o_ref[:, cs] = o.astype(BF16)


def _mem_attn(proj, kvm, *, S, tm=512):
    T = proj.shape[0]
    W = X_HEADS * X_HEAD_DIM
    nsb = S // tm
    return pl.pallas_call(
        _mem_attn_kernel,
        grid=(T // tm,),
        in_specs=[
            pl.BlockSpec((tm, W), lambda i: (i, _PROJ_XQ // W)),
            pl.BlockSpec((N_MEM, 2 * W), lambda i: (i // nsb, 0)),
        ],
        out_specs=pl.BlockSpec((tm, W), lambda i: (i, 0)),
        out_shape=jax.ShapeDtypeStruct((T, W), BF16),
        compiler_params=_cparams(1),
        name="mem_attn",
    )(proj, kvm)


def _merge_kernel(gate_ref, omla_ref, od0_ref, od1_ref, od2_ref, ls0_ref, ls1_ref, ls2_ref, omem_ref,
                  wmla_ref, wdil_ref, wmem_ref, o_ref):
    ls = [ls0_ref[...], ls1_ref[...], ls2_ref[...]]
    mx = jnp.maximum(jnp.maximum(ls[0], ls[1]), ls[2])
    es = [jnp.exp(x - mx) for x in ls]
    inv = 1.0 / (es[0] + es[1] + es[2])
    od = [od0_ref, od1_ref, od2_ref]
    parts = []
    for h in range(DIL_HEADS_PER_GROUP):
        cs = slice(h * DIL_HEAD_DIM, (h + 1) * DIL_HEAD_DIM)
        acc = None
        for gi in range(DIL_GROUPS):
            a = (es[gi] * inv)[:, h * 32:h * 32 + 1]
            term = a * od[gi][:, cs].astype(F32)
            acc = term if acc is None else acc + term
        parts.append(acc.astype(BF16))
    o_dil = jnp.concatenate(parts, axis=-1)

    def gate(b):
        return jax.nn.sigmoid(gate_ref[:, b * D_MODEL:(b + 1) * D_MODEL].astype(F32))

    merged = gate(0) * _dot(omla_ref[...], wmla_ref[...])
    merged += gate(1) * _dot(o_dil, wdil_ref[...])
    merged += gate(2) * _dot(omem_ref[...], wmem_ref[...])
    o_ref[...] = merged.astype(BF16)


def _merge(proj, o_mla, ods, lss, o_mem, wmla, wdil, wmem, *, tm=512):
    T = proj.shape[0]
    row = lambda i: (i, 0)
    const = lambda i: (0, 0)
    wspec = lambda w: pl.BlockSpec(w.shape, const, pipeline_mode=pl.Buffered(1))
    return pl.pallas_call(
        _merge_kernel,
        grid=(T // tm,),
        in_specs=[
            pl.BlockSpec((tm, N_BRANCH * D_MODEL), row),
            pl.BlockSpec((tm, o_mla.shape[1]), row),
            pl.BlockSpec((tm, DIL_W), row), pl.BlockSpec((tm, DIL_W), row), pl.BlockSpec((tm, DIL_W), row),
            pl.BlockSpec((tm, 128), row), pl.BlockSpec((tm, 128), row), pl.BlockSpec((tm, 128), row),
            pl.BlockSpec((tm, o_mem.shape[1]), row),
            wspec(wmla), wspec(wdil), wspec(wmem),
        ],
        out_specs=pl.BlockSpec((tm, D_MODEL), row),
        out_shape=jax.ShapeDtypeStruct((T, D_MODEL), BF16),
        compiler_params=_cparams(1),
        name="merge",
    )(proj, o_mla, *ods, *lss, o_mem, wmla, wdil, wmem)


def _out_mlp_kernel(x_ref, mg_ref, wout_ref, gm_ref, wup_ref, wdn_ref, gf_ref, o_ref, h_ref):
    j = pl.program_id(1)

    @pl.when(j == 0)
    def _():
        x1 = x_ref[...] + _dot(mg_ref[...], wout_ref[...])
        o_ref[...] = x1
        h_ref[...] = _rms(x1, gm_ref[...]).astype(BF16)

    u = jnp.maximum(_dot(h_ref[...], wup_ref[...]), 0.0)
    o_ref[...] += _dot((u * u).astype(BF16), wdn_ref[...])

    @pl.when(j == pl.num_programs(1) - 1)
    def _():
        o_ref[...] = _rms(o_ref[...], gf_ref[...])


def _out_mlp(x2, merged, wout, gm, wup, wdn, gf, *, tm=512, tf=1024):
    T, D = x2.shape
    row = lambda i, j: (i, 0)
    const = lambda i, j: (0, 0)
    return pl.pallas_call(
        _out_mlp_kernel,
        grid=(T // tm, D_FF // tf),
        in_specs=[
            pl.BlockSpec((tm, D), row),
            pl.BlockSpec((tm, D), row),
            pl.BlockSpec((D, D), const, pipeline_mode=pl.Buffered(1)),
            pl.BlockSpec((1, D), const),
            pl.BlockSpec((D, tf), lambda i, j: (0, j)),
            pl.BlockSpec((tf, D), lambda i, j: (j, 0)),
            pl.BlockSpec((1, D), const),
        ],
        out_specs=pl.BlockSpec((tm, D), row),
        out_shape=jax.ShapeDtypeStruct((T, D), F32),
        scratch_shapes=[pltpu.VMEM((tm, D), BF16)],
        compiler_params=_cparams(2),
        name="out_mlp",
    )(x2, merged, wout, gm, wup, wdn, gf)


def _rot_half_cols(w):
    half = QK_ROPE // 2
    return jnp.concatenate([-w[..., half:], w[..., :half]], axis=-1)


def _prep_weights(w_in, w_uq, w_ukv, w_mem_kv, w_b_mla, w_b_dil, w_b_mem, w_out, w_up, w_down):
    D = w_in.shape[0]
    w_in = w_in.astype(BF16)
    o = 0
    cq = w_in[:, o:o + Q_LORA]; o += Q_LORA
    ckv = w_in[:, o:o + KV_LORA]; o += KV_LORA
    kr = w_in[:, o:o + QK_ROPE]; o += QK_ROPE
    dil = w_in[:, o:o + 3 * DIL_GROUPS * DIL_W]; o += 3 * DIL_GROUPS * DIL_W
    xq = w_in[:, o:o + X_HEADS * X_HEAD_DIM]; o += X_HEADS * X_HEAD_DIM
    gate = w_in[:, o:o + N_BRANCH * D_MODEL]
    dil = dil.reshape(D, 3, DIL_GROUPS, DIL_W).transpose(0, 2, 1, 3).reshape(D, 3 * DIL_GROUPS * DIL_W)
    z64 = jnp.zeros((D, 64), BF16)
    w_in_p = jnp.concatenate(
        [gate, xq, dil, cq, ckv, kr, z64, _rot_half_cols(kr), z64, jnp.zeros((D, 256), BF16)], axis=1)

    wkv = w_ukv.reshape(KV_LORA, MLA_HEADS, QK_NOPE + V_HEAD)
    wk = wkv[..., :QK_NOPE].reshape(KV_LORA, MLA_HEADS * QK_NOPE).astype(BF16)
    wvt = wkv[..., QK_NOPE:].reshape(KV_LORA, MLA_HEADS * V_HEAD).T.astype(BF16)
    return dict(w_in=w_in_p, wqt=w_uq.T.astype(BF16), wk=wk, wvt=wvt, wmkv=w_mem_kv.astype(BF16),
                wmla=w_b_mla.astype(BF16), wdil=w_b_dil.astype(BF16), wmem=w_b_mem.astype(BF16),
                wout=w_out.astype(BF16), wup=w_up.astype(BF16), wdn=w_down.astype(BF16))


def _rope_tables(S):
    half = QK_ROPE // 2
    inv = 1.0 / (ROPE_THETA ** (jnp.arange(half, dtype=F32) / half))
    ang = jnp.arange(S).astype(F32)[:, None] * inv[None, :]
    z = jnp.zeros((S, 64), F32)
    cos, sin = jnp.cos(ang), jnp.sin(ang)
    return jnp.concatenate([cos, cos, z], axis=1), jnp.concatenate([sin, sin, z], axis=1), cos.T, sin.T


def _trunk(x, mem, W, bias_tabs, g_attn, g_qn, g_kvn, g_mem, g_mlp, g_final):
    B, S, D = x.shape
    T = B * S
    x2 = x.reshape(T, D)
    row = lambda g: g.reshape(1, -1).astype(F32)

    proj = _in_proj(x2, row(g_attn), W["w_in"])

    q_scale = (QK_NOPE + QK_ROPE) ** -0.5 * math.log2(math.e)
    qt, k, vt = _mla_proj(proj, _rope_tables(S), row(g_qn), row(g_kvn), W["wqt"], W["wk"], W["wvt"], S=S,
                          q_scale=q_scale)
    o_mla = _mla_flash(qt, k, vt, B=B, S=S)

    ods, lss = _dilated(proj, bias_tabs, B=B, S=S)

    kvm = _mem_kv(mem.reshape(B * N_MEM, D), row(g_mem), W["wmkv"])
    o_mem = _mem_attn(proj, kvm, S=S)

    merged = _merge(proj, o_mla, ods, lss, o_mem, W["wmla"], W["wdil"], W["wmem"])
    y = _out_mlp(x2, merged, W["wout"], row(g_mlp), W["wup"], W["wdn"], row(g_final))
    return y.reshape(B, S, D)


def kernel(x_prompt, x_sample, mem_prompt, mem_sample, rel_bias, g_attn, w_in, g_q_norm, w_uq, g_kv_norm, w_ukv,
           g_mem, w_mem_kv, w_b_mla, w_b_dil, w_b_mem, w_out, g_mlp, w_up, w_down, g_final):
    assert w_in.shape[0] == 1, "single layer"
    W = _prep_weights(w_in[0], w_uq[0], w_ukv[0], w_mem_kv[0], w_b_mla[0], w_b_dil[0], w_b_mem[0], w_out[0],
                      w_up[0], w_down[0])
    bias_tabs = _dil_bias_tables(rel_bias)
    args = (W, bias_tabs, g_attn[0], g_q_norm[0], g_kv_norm[0], g_mem[0], g_mlp[0], g_final)
    return (_trunk(x_prompt, mem_prompt, *args), _trunk(x_sample, mem_sample, *args))
```

```python
import functools
import math

import numpy as np
import jax
import jax.numpy as jnp
from jax import lax
from jax.experimental import pallas as pl
from jax.experimental.pallas import tpu as pltpu

F32 = jnp.float32
BF16 = jnp.bfloat16

D_MODEL = 2048
N_MEM = 256
MLA_HEADS = 12
Q_LORA = 512
KV_LORA = 512
QK_NOPE = 128
QK_ROPE = 64
V_HEAD = 128
ROPE_THETA = 10000.0
DIL_PAIRS = ((128, 1), (512, 4), (2048, 16))
DIL_GROUPS = 3
DIL_HEADS_PER_GROUP = 4
DIL_HEAD_DIM = 128
X_HEADS = 4
X_HEAD_DIM = 256
NUM_BUCKETS = 32
MAX_DISTANCE = 1024
D_FF = 4 * D_MODEL
N_BRANCH = 3
EPS = 1e-6
NEG_INF = -1e30

DIL_W = DIL_HEADS_PER_GROUP * DIL_HEAD_DIM
DIL_SIDE = 64
DIL_QB = 128
DIL_KB = DIL_QB + 2 * DIL_SIDE
QK_PAD = 256
VT_ROWS = V_HEAD + 16

_PROJ_GATE = 0
_PROJ_XQ = _PROJ_GATE + N_BRANCH * D_MODEL
_PROJ_DIL = _PROJ_XQ + X_HEADS * X_HEAD_DIM
_PROJ_CQ = _PROJ_DIL + 3 * DIL_GROUPS * DIL_W
_PROJ_CKV = _PROJ_CQ + Q_LORA
_PROJ_KR = _PROJ_CKV + KV_LORA
PROJ_W = _PROJ_KR + 512

VMEM_LIMIT = 52 * 1024 * 1024


def _cparams(n_axes):
    return pltpu.CompilerParams(dimension_semantics=("arbitrary",) * n_axes, vmem_limit_bytes=VMEM_LIMIT)


def _rms(x32, g32):
    return x32 * lax.rsqrt(jnp.mean(x32 * x32, axis=-1, keepdims=True) + EPS) * g32


def _dot(a, b):
    return jnp.dot(a, b, preferred_element_type=F32)


def _dot_nt(a, b):
    return lax.dot_general(a, b, (((1,), (1,)), ((), ())), preferred_element_type=F32)


def _in_proj_kernel(x_ref, g_ref, w_ref, o_ref, h_ref):
    @pl.when(pl.program_id(1) == 0)
    def _():
        h_ref[...] = _rms(x_ref[...], g_ref[...]).astype(BF16)

    o_ref[...] = _dot(h_ref[...], w_ref[...]).astype(BF16)


def _in_proj(x2, g, w, *, tm=1024, tn=1024):
    T, D = x2.shape
    return pl.pallas_call(
        _in_proj_kernel,
        grid=(T // tm, PROJ_W // tn),
        in_specs=[
            pl.BlockSpec((tm, D), lambda i, j: (i, 0)),
            pl.BlockSpec((1, D), lambda i, j: (0, 0)),
            pl.BlockSpec((D, tn), lambda i, j: (0, j)),
        ],
        out_specs=pl.BlockSpec((tm, tn), lambda i, j: (i, j)),
        out_shape=jax.ShapeDtypeStruct((T, PROJ_W), BF16),
        scratch_shapes=[pltpu.VMEM((tm, D), BF16)],
        compiler_params=_cparams(2),
        name="in_proj",
    )(x2, g, w)


def _mla_proj_kernel(cq_ref, ckv_ref, kr_ref, cos_ref, sin_ref, cost_ref, sint_ref, gq_ref, gkv_ref,
                     wqt_ref, wk_ref, wvt_ref, qt_ref, k_ref, vt_ref, *, q_scale):
    half = QK_ROPE // 2
    hd = QK_NOPE + QK_ROPE
    nq_t = _rms(cq_ref[...].astype(F32), gq_ref[...]).T.astype(BF16)
    q_t = _dot(wqt_ref[...], nq_t)
    cos_t = cost_ref[...]
    sin_t = sint_ref[...]
    zeros = jnp.zeros((QK_PAD - hd, q_t.shape[1]), BF16)
    for h in range(MLA_HEADS):
        r0, o0 = h * hd, h * QK_PAD
        x1 = q_t[r0 + QK_NOPE:r0 + QK_NOPE + half]
        x2 = q_t[r0 + QK_NOPE + half:r0 + hd]
        qt_ref[o0:o0 + QK_NOPE, :] = (q_t[r0:r0 + QK_NOPE] * q_scale).astype(BF16)
        qt_ref[o0 + QK_NOPE:o0 + QK_NOPE + half, :] = ((x1 * cos_t - x2 * sin_t) * q_scale).astype(BF16)
        qt_ref[o0 + QK_NOPE + half:o0 + hd, :] = ((x2 * cos_t + x1 * sin_t) * q_scale).astype(BF16)
        qt_ref[o0 + hd:o0 + QK_PAD, :] = zeros

    nkv = _rms(ckv_ref[...].astype(F32), gkv_ref[...])
    v_t = _dot(wvt_ref[...], nkv.T.astype(BF16)).astype(BF16)
    extra = jnp.where(lax.broadcasted_iota(jnp.int32, (VT_ROWS - V_HEAD, v_t.shape[1]), 0) == 0, 1.0, 0.0).astype(BF16)
    for h in range(MLA_HEADS):
        vt_ref[h * VT_ROWS:h * VT_ROWS + V_HEAD, :] = v_t[h * V_HEAD:(h + 1) * V_HEAD]
        vt_ref[h * VT_ROWS + V_HEAD:(h + 1) * VT_ROWS, :] = extra
    kn = _dot(nkv.astype(BF16), wk_ref[...])
    kr = kr_ref[...].astype(F32)
    kpe = (kr[:, 0:128] * cos_ref[...] + kr[:, 128:256] * sin_ref[...]).astype(BF16)
    for h in range(MLA_HEADS):
        o0 = h * QK_PAD
        k_ref[:, o0:o0 + QK_NOPE] = kn[:, h * QK_NOPE:(h + 1) * QK_NOPE].astype(BF16)
        k_ref[:, o0 + QK_NOPE:o0 + QK_PAD] = kpe


def _mla_proj(proj, rope, gq, gkv, wqt, wk, wvt, *, S, q_scale, tm=512):
    T = proj.shape[0]
    nsb = S // tm
    HQ = MLA_HEADS * QK_PAD
    HV = MLA_HEADS * VT_ROWS
    cos_k, sin_k, cos_t, sin_t = rope
    const = lambda i: (0, 0)
    wspec = lambda w: pl.BlockSpec(w.shape, const, pipeline_mode=pl.Buffered(1))
    return pl.pallas_call(
        functools.partial(_mla_proj_kernel, q_scale=q_scale),
        grid=(T // tm,),
        in_specs=[
            pl.BlockSpec((tm, Q_LORA), lambda i: (i, _PROJ_CQ // 512)),
            pl.BlockSpec((tm, KV_LORA), lambda i: (i, _PROJ_CKV // 512)),
            pl.BlockSpec((tm, 512), lambda i: (i, _PROJ_KR // 512)),
            pl.BlockSpec((tm, 128), lambda i: (i % nsb, 0)),
            pl.BlockSpec((tm, 128), lambda i: (i % nsb, 0)),
            pl.BlockSpec((QK_ROPE // 2, tm), lambda i: (0, i % nsb)),
            pl.BlockSpec((QK_ROPE // 2, tm), lambda i: (0, i % nsb)),
            pl.BlockSpec((1, Q_LORA), const),
            pl.BlockSpec((1, KV_LORA), const),
            wspec(wqt), wspec(wk), wspec(wvt),
        ],
        out_specs=[
            pl.BlockSpec((HQ, tm), lambda i: (0, i)),
            pl.BlockSpec((tm, HQ), lambda i: (i, 0)),
            pl.BlockSpec((HV, tm), lambda i: (0, i)),
        ],
        out_shape=[
            jax.ShapeDtypeStruct((HQ, T), BF16),
            jax.ShapeDtypeStruct((T, HQ), BF16),
            jax.ShapeDtypeStruct((HV, T), BF16),
        ],
        compiler_params=_cparams(1),
        name="mla_proj",
    )(proj, proj, proj, cos_k, sin_k, cos_t, sin_t, gq, gkv, wqt, wk, wvt)


FLASH_SLAB = 32
FLASH_QK_ROWS = 256


FLASH_WINDOW = 100.0
FLASH_PV_AFTER = 4
FLASH_EXACT_ROWS = 512


def _zero_like_bits(v):
    u = pltpu.bitcast(v, jnp.uint32)
    return pltpu.bitcast(lax.shift_right_logical(lax.shift_right_logical(u, jnp.uint32(16)), jnp.uint32(16)), F32)


def _mla_flash_kernel(qt_ref, k_ref, vt_ref, o_ref, p0_ref, p1_ref, acc_ref, *, tq, tk, nk, nq):
    p_refs = (p0_ref, p1_ref)
    nc = nq * nk

    def chunk_pos(c):
        qi = c // nk
        return qi, c == qi * nk, pl.multiple_of((c - qi * nk) * tk, tk), pl.multiple_of(qi * tq, tq)

    def pv(c, slot):
        _, first, r0, _ = chunk_pos(c)
        acc = jnp.where(first, 0.0, acc_ref[...]) + _dot(vt_ref[:, pl.ds(r0, tk)], p_refs[slot][...])
        acc_ref[...] = acc
        return acc[0:1]

    def qk_exp(c, slot, st, pv_args=None):
        ref, rmax, done_ref, done_rmax = st
        _, first, r0, c0 = chunk_pos(c)
        done_ref = jnp.where(first, ref, done_ref)
        done_rmax = jnp.where(first, rmax, done_rmax)
        ref = jnp.where(first, rmax, ref)
        rmax = jnp.where(first, -jnp.inf, rmax)
        mx = None
        shift = ref
        for n, q0 in enumerate(range(0, tk, FLASH_QK_ROWS)):
            if n == FLASH_PV_AFTER and pv_args is not None:
                shift = ref + _zero_like_bits(pv(*pv_args))
            s = _dot(k_ref[pl.ds(r0 + q0, FLASH_QK_ROWS), :], qt_ref[:, pl.ds(c0, tq)])
            p_refs[slot][q0:q0 + FLASH_QK_ROWS, :] = jnp.exp2(s - shift).astype(BF16)
            part = jnp.max(s.reshape(FLASH_QK_ROWS // FLASH_SLAB, FLASH_SLAB, tq), axis=0)
            mx = part if mx is None else jnp.maximum(mx, part)
        rmax = jnp.maximum(rmax, jnp.max(mx, axis=0, keepdims=True))
        return ref, rmax, done_ref, done_rmax

    def exact_block(qi):
        qt = qt_ref[:, pl.ds(pl.multiple_of(qi * tq, tq), tq)]

        def step(j, carry):
            m, acc = carry
            r0 = pl.multiple_of(j * FLASH_EXACT_ROWS, FLASH_EXACT_ROWS)
            s = _dot(k_ref[pl.ds(r0, FLASH_EXACT_ROWS), :], qt)
            m_new = jnp.maximum(m, jnp.max(s, axis=0, keepdims=True))
            p = jnp.exp2(s - m_new).astype(BF16)
            return m_new, acc * jnp.exp2(m - m_new) + _dot(vt_ref[:, pl.ds(r0, FLASH_EXACT_ROWS)], p)

        init = (jnp.full((1, tq), -jnp.inf, F32), jnp.zeros((VT_ROWS, tq), F32))
        _, acc = lax.fori_loop(0, nk * tk // FLASH_EXACT_ROWS, step, init)
        acc_ref[...] = acc

    def finalize(c, ref_b, rmax_b):
        qi = c // nk
        in_window = jnp.max(jnp.abs(rmax_b - ref_b)) <= FLASH_WINDOW

        @pl.when(jnp.logical_not(in_window))
        def _():
            exact_block(qi)

        acc = acc_ref[...]
        o = acc[:V_HEAD] * (1.0 / acc[V_HEAD:V_HEAD + 1])
        o_ref[pl.ds(pl.multiple_of(qi * tq, tq), tq), :] = o.T.astype(BF16)

    mx = None
    for q0 in range(0, tk, FLASH_QK_ROWS):
        s = _dot(k_ref[q0:q0 + FLASH_QK_ROWS, :], qt_ref[:, 0:tq])
        part = jnp.max(s.reshape(FLASH_QK_ROWS // FLASH_SLAB, FLASH_SLAB, tq), axis=0)
        mx = part if mx is None else jnp.maximum(mx, part)
    first_max = jnp.max(mx, axis=0, keepdims=True)
    ninf = jnp.full((1, tq), -jnp.inf, F32)
    st = qk_exp(0, 0, (first_max, first_max, ninf, ninf))

    def chunks_of_block(i, st, n):
        for j in range(n):
            c = i * nk + j
            st = qk_exp(c + 1, (j + 1) % 2, st, (c, j % 2))
        return st

    def block(i, st):
        st = chunks_of_block(i, st, nk)
        finalize(i * nk + nk - 1, st[2], st[3])
        return st

    st = lax.fori_loop(0, nq - 1, block, st)
    st = chunks_of_block(nq - 1, st, nk - 1)
    pv(nc - 1, (nk - 1) % 2)
    finalize(nc - 1, st[0], st[1])


def _mla_flash(qt, k, vt, *, B, S, tq=512, tk=2048):
    T = B * S
    nq, nk = S // tq, S // tk
    assert nk % 2 == 0 and tk // FLASH_QK_ROWS > FLASH_PV_AFTER
    return pl.pallas_call(
        functools.partial(_mla_flash_kernel, tq=tq, tk=tk, nk=nk, nq=nq),
        grid=(B, MLA_HEADS),
        in_specs=[
            pl.BlockSpec((QK_PAD, S), lambda b, h: (h, b)),
            pl.BlockSpec((S, QK_PAD), lambda b, h: (b, h)),
            pl.BlockSpec((VT_ROWS, S), lambda b, h: (h, b)),
        ],
        out_specs=pl.BlockSpec((S, V_HEAD), lambda b, h: (b, h)),
        out_shape=jax.ShapeDtypeStruct((T, MLA_HEADS * V_HEAD), BF16),
        scratch_shapes=[pltpu.VMEM((tk, tq), BF16), pltpu.VMEM((tk, tq), BF16), pltpu.VMEM((VT_ROWS, tq), F32)],
        compiler_params=_cparams(2),
        name="mla_flash",
    )(qt, k, vt)


def _dil_kernel(q_ref, k_ref, v_ref, bias_ref, o_ref, lse_ref, *, L, ta):
    scale = DIL_HEAD_DIM ** -0.5
    lane = lax.broadcasted_iota(jnp.int32, (DIL_QB, 128), 1)
    base = pl.program_id(2) * ta
    for t in range(ta // DIL_QB):
        a0 = base + t * DIL_QB
        k0 = jnp.clip(a0 - DIL_SIDE, 0, L - DIL_KB)
        var = (a0 - k0) // DIL_SIDE
        k0 = pl.multiple_of(k0, DIL_SIDE)
        rows = slice(t * DIL_QB, (t + 1) * DIL_QB)
        lses = []
        for h in range(DIL_HEADS_PER_GROUP):
            cs = slice(h * DIL_HEAD_DIM, (h + 1) * DIL_HEAD_DIM)
            qh = q_ref[0, 0, rows, cs]
            kh = k_ref[0, 0, pl.ds(k0, DIL_KB), cs]
            vh = v_ref[0, 0, pl.ds(k0, DIL_KB), cs]
            s = _dot_nt(qh, kh) * scale + bias_ref[var, h]
            m = jnp.max(s, axis=-1, keepdims=True)
            p = jnp.exp(s - m)
            l = jnp.sum(p, axis=-1, keepdims=True)
            o_ref[0, 0, rows, cs] = (_dot(p.astype(BF16), vh) / l).astype(BF16)
            lses.append(m + jnp.log(l))
        packed = jnp.where(lane < 32, lses[0], jnp.where(lane < 64, lses[1], jnp.where(lane < 96, lses[2], lses[3])))
        lse_ref[0, 0, rows, :] = packed


def _dilated_group(q, k, v, bias, *, qcol, kcol, vcol, name):
    B, r, L, _ = q.shape
    ta = min(L, 512)
    return pl.pallas_call(
        functools.partial(_dil_kernel, L=L, ta=ta),
        grid=(B, r, L // ta),
        in_specs=[
            pl.BlockSpec((1, 1, ta, DIL_W), lambda b, c, a: (b, c, a, qcol)),
            pl.BlockSpec((1, 1, L, DIL_W), lambda b, c, a: (b, c, 0, kcol)),
            pl.BlockSpec((1, 1, L, DIL_W), lambda b, c, a: (b, c, 0, vcol)),
            pl.BlockSpec(bias.shape, lambda b, c, a: (0, 0, 0, 0)),
        ],
        out_specs=[
            pl.BlockSpec((1, 1, ta, DIL_W), lambda b, c, a: (b, c, a, 0)),
            pl.BlockSpec((1, 1, ta, 128), lambda b, c, a: (b, c, a, 0)),
        ],
        out_shape=[
            jax.ShapeDtypeStruct((B, r, L, DIL_W), BF16),
            jax.ShapeDtypeStruct((B, r, L, 128), F32),
        ],
        compiler_params=_cparams(3),
        name=name,
    )(q, k, v, bias)


def _dilated(proj, bias_tabs, *, B, S):
    T = B * S
    ods, lss = [], []
    for g, (_, r) in enumerate(DIL_PAIRS):
        L = S // r
        c0 = _PROJ_DIL + g * 3 * DIL_W
        if r == 1:
            x, col = proj.reshape(B, 1, S, PROJ_W), c0 // DIL_W
        else:
            x = proj.reshape(B, S, PROJ_W)[:, :, c0:c0 + 3 * DIL_W].reshape(B, L, r, 3 * DIL_W)
            x, col = jnp.einsum("blrw->brlw", x), 0
        od, ls = _dilated_group(x, x, x, bias_tabs[g], qcol=col, kcol=col + 1, vcol=col + 2, name=f"dilated_g{g}")
        if r != 1:
            od = jnp.einsum("brlw->blrw", od)
            ls = jnp.einsum("brlw->blrw", ls)
        ods.append(od.reshape(T, DIL_W))
        lss.append(ls.reshape(T, 128))
    return ods, lss


def _t5_bucket(rel):
    nb = NUM_BUCKETS // 2
    ret = (rel > 0).astype(np.int32) * nb
    n = np.abs(rel)
    max_exact = nb // 2
    large = max_exact + (np.log(np.maximum(n, 1) / max_exact) / np.log(MAX_DISTANCE / max_exact)
                         * (nb - max_exact)).astype(np.int32)
    large = np.minimum(large, nb - 1)
    return (ret + np.where(n < max_exact, n, large)).astype(np.int32)


def _dil_bias_tables(rel_bias):
    qq = np.arange(DIL_QB)[:, None]
    kk = np.arange(DIL_KB)[None, :]
    ext = jnp.concatenate([rel_bias.astype(F32), jnp.full((1, rel_bias.shape[1]), NEG_INF, F32)], axis=0)
    tables = []
    for g, (_, r) in enumerate(DIL_PAIRS):
        idx = []
        for var in range(3):
            j = kk - qq - var * DIL_SIDE
            bucket = _t5_bucket(j * r)
            idx.append(np.where(np.abs(j) <= DIL_SIDE, bucket, NUM_BUCKETS))
        onehot = (np.stack(idx)[..., None] == np.arange(NUM_BUCKETS + 1)).astype(np.float32)
        heads = ext[:, g * DIL_HEADS_PER_GROUP:(g + 1) * DIL_HEADS_PER_GROUP]
        tables.append(jnp.einsum("vqkn,nh->vhqk", onehot, heads, precision=lax.Precision.HIGHEST))
    return tables


def _mem_kv_kernel(m_ref, g_ref, w_ref, o_ref):
    o_ref[...] = _dot(_rms(m_ref[...], g_ref[...]).astype(BF16), w_ref[...]).astype(BF16)


def _mem_kv(mem2, g, w):
    R, D = mem2.shape
    N = w.shape[1]
    return pl.pallas_call(
        _mem_kv_kernel,
        grid=(R // N_MEM,),
        in_specs=[
            pl.BlockSpec((N_MEM, D), lambda i: (i, 0)),
            pl.BlockSpec((1, D), lambda i: (0, 0)),
            pl.BlockSpec((D, N), lambda i: (0, 0)),
        ],
        out_specs=pl.BlockSpec((N_MEM, N), lambda i: (i, 0)),
        out_shape=jax.ShapeDtypeStruct((R, N), BF16),
        compiler_params=_cparams(1),
        name="mem_kv",
    )(mem2, g, w)


def _mem_attn_kernel(q_ref, kv_ref, o_ref):
    scale = X_HEAD_DIM ** -0.5
    hv = X_HEADS * X_HEAD_DIM
    for h in range(X_HEADS):
        cs = slice(h * X_HEAD_DIM, (h + 1) * X_HEAD_DIM)
        s = _dot_nt(q_ref[:, cs], kv_ref[:, cs]) * scale
        m = jnp.max(s, axis=-1, keepdims=True)
        p = jnp.exp(s - m)
        l = jnp.sum(p, axis=-1, keepdims=True)
        o = _dot(p.astype(BF16), kv_ref[:, hv + h * X_HEAD_DIM:hv + (h + 1) * X_HEAD_DIM]) / l
        o_ref[:, cs] = o.astype(BF16)


def _mem_attn(proj, kvm, *, S, tm=512):
    T = proj.shape[0]
    W = X_HEADS * X_HEAD_DIM
    nsb = S // tm
    return pl.pallas_call(
        _mem_attn_kernel,
        grid=(T // tm,),
        in_specs=[
            pl.BlockSpec((tm, W), lambda i: (i, _PROJ_XQ // W)),
            pl.BlockSpec((N_MEM, 2 * W), lambda i: (i // nsb, 0)),
        ],
        out_specs=pl.BlockSpec((tm, W), lambda i: (i, 0)),
        out_shape=jax.ShapeDtypeStruct((T, W), BF16),
        compiler_params=_cparams(1),
        name="mem_attn",
    )(proj, kvm)


def _merge_kernel(gate_ref, omla_ref, od0_ref, od1_ref, od2_ref, ls0_ref, ls1_ref, ls2_ref, omem_ref,
                  wmla_ref, wdil_ref, wmem_ref, o_ref):
    ls = [ls0_ref[...], ls1_ref[...], ls2_ref[...]]
    mx = jnp.maximum(jnp.maximum(ls[0], ls[1]), ls[2])
    es = [jnp.exp(x - mx) for x in ls]
    inv = 1.0 / (es[0] + es[1] + es[2])
    od = [od0_ref, od1_ref, od2_ref]
    parts = []
    for h in range(DIL_HEADS_PER_GROUP):
        cs = slice(h * DIL_HEAD_DIM, (h + 1) * DIL_HEAD_DIM)
        acc = None
        for gi in range(DIL_GROUPS):
            a = (es[gi] * inv)[:, h * 32:h * 32 + 1]
            term = a * od[gi][:, cs].astype(F32)
            acc = term if acc is None else acc + term
        parts.append(acc.astype(BF16))
    o_dil = jnp.concatenate(parts, axis=-1)

    def gate(b):
        return jax.nn.sigmoid(gate_ref[:, b * D_MODEL:(b + 1) * D_MODEL].astype(F32))

    merged = gate(0) * _dot(omla_ref[...], wmla_ref[...])
    merged += gate(1) * _dot(o_dil, wdil_ref[...])
    merged += gate(2) * _dot(omem_ref[...], wmem_ref[...])
    o_ref[...] = merged.astype(BF16)


def _merge(proj, o_mla, ods, lss, o_mem, wmla, wdil, wmem, *, tm=512):
    T = proj.shape[0]
    row = lambda i: (i, 0)
    const = lambda i: (0, 0)
    wspec = lambda w: pl.BlockSpec(w.shape, const, pipeline_mode=pl.Buffered(1))
    return pl.pallas_call(
        _merge_kernel,
        grid=(T // tm,),
        in_specs=[
            pl.BlockSpec((tm, N_BRANCH * D_MODEL), row),
            pl.BlockSpec((tm, o_mla.shape[1]), row),
            pl.BlockSpec((tm, DIL_W), row), pl.BlockSpec((tm, DIL_W), row), pl.BlockSpec((tm, DIL_W), row),
            pl.BlockSpec((tm, 128), row), pl.BlockSpec((tm, 128), row), pl.BlockSpec((tm, 128), row),
            pl.BlockSpec((tm, o_mem.shape[1]), row),
            wspec(wmla), wspec(wdil), wspec(wmem),
        ],
        out_specs=pl.BlockSpec((tm, D_MODEL), row),
        out_shape=jax.ShapeDtypeStruct((T, D_MODEL), BF16),
        compiler_params=_cparams(1),
        name="merge",
    )(proj, o_mla, *ods, *lss, o_mem, wmla, wdil, wmem)


def _out_mlp_kernel(x_ref, mg_ref, wout_ref, gm_ref, wup_ref, wdn_ref, gf_ref, o_ref, h_ref):
    j = pl.program_id(1)

    @pl.when(j == 0)
    def _():
        x1 = x_ref[...] + _dot(mg_ref[...], wout_ref[...])
        o_ref[...] = x1
        h_ref[...] = _rms(x1, gm_ref[...]).astype(BF16)

    u = jnp.maximum(_dot(h_ref[...], wup_ref[...]), 0.0)
    o_ref[...] += _dot((u * u).astype(BF16), wdn_ref[...])

    @pl.when(j == pl.num_programs(1) - 1)
    def _():
        o_ref[...] = _rms(o_ref[...], gf_ref[...])


def _out_mlp(x2, merged, wout, gm, wup, wdn, gf, *, tm=512, tf=1024):
    T, D = x2.shape
    row = lambda i, j: (i, 0)
    const = lambda i, j: (0, 0)
    return pl.pallas_call(
        _out_mlp_kernel,
        grid=(T // tm, D_FF // tf),
        in_specs=[
            pl.BlockSpec((tm, D), row),
            pl.BlockSpec((tm, D), row),
            pl.BlockSpec((D, D), const, pipeline_mode=pl.Buffered(1)),
            pl.BlockSpec((1, D), const),
            pl.BlockSpec((D, tf), lambda i, j: (0, j)),
            pl.BlockSpec((tf, D), lambda i, j: (j, 0)),
            pl.BlockSpec((1, D), const),
        ],
        out_specs=pl.BlockSpec((tm, D), row),
        out_shape=jax.ShapeDtypeStruct((T, D), F32),
        scratch_shapes=[pltpu.VMEM((tm, D), BF16)],
        compiler_params=_cparams(2),
        name="out_mlp",
    )(x2, merged, wout, gm, wup, wdn, gf)


def _rot_half_cols(w):
    half = QK_ROPE // 2
    return jnp.concatenate([-w[..., half:], w[..., :half]], axis=-1)


def _prep_weights(w_in, w_uq, w_ukv, w_mem_kv, w_b_mla, w_b_dil, w_b_mem, w_out, w_up, w_down):
    D = w_in.shape[0]
    w_in = w_in.astype(BF16)
    o = 0
    cq = w_in[:, o:o + Q_LORA]; o += Q_LORA
    ckv = w_in[:, o:o + KV_LORA]; o += KV_LORA
    kr = w_in[:, o:o + QK_ROPE]; o += QK_ROPE
    dil = w_in[:, o:o + 3 * DIL_GROUPS * DIL_W]; o += 3 * DIL_GROUPS * DIL_W
    xq = w_in[:, o:o + X_HEADS * X_HEAD_DIM]; o += X_HEADS * X_HEAD_DIM
    gate = w_in[:, o:o + N_BRANCH * D_MODEL]
    dil = dil.reshape(D, 3, DIL_GROUPS, DIL_W).transpose(0, 2, 1, 3).reshape(D, 3 * DIL_GROUPS * DIL_W)
    z64 = jnp.zeros((D, 64), BF16)
    w_in_p = jnp.concatenate(
        [gate, xq, dil, cq, ckv, kr, z64, _rot_half_cols(kr), z64, jnp.zeros((D, 256), BF16)], axis=1)

    wkv = w_ukv.reshape(KV_LORA, MLA_HEADS, QK_NOPE + V_HEAD)
    wk = wkv[..., :QK_NOPE].reshape(KV_LORA, MLA_HEADS * QK_NOPE).astype(BF16)
    wvt = wkv[..., QK_NOPE:].reshape(KV_LORA, MLA_HEADS * V_HEAD).T.astype(BF16)
    return dict(w_in=w_in_p, wqt=w_uq.T.astype(BF16), wk=wk, wvt=wvt, wmkv=w_mem_kv.astype(BF16),
                wmla=w_b_mla.astype(BF16), wdil=w_b_dil.astype(BF16), wmem=w_b_mem.astype(BF16),
                wout=w_out.astype(BF16), wup=w_up.astype(BF16), wdn=w_down.astype(BF16))


def _rope_tables(S):
    half = QK_ROPE // 2
    inv = 1.0 / (ROPE_THETA ** (jnp.arange(half, dtype=F32) / half))
    ang = jnp.arange(S).astype(F32)[:, None] * inv[None, :]
    z = jnp.zeros((S, 64), F32)
    cos, sin = jnp.cos(ang), jnp.sin(ang)
    return jnp.concatenate([cos, cos, z], axis=1), jnp.concatenate([sin, sin, z], axis=1), cos.T, sin.T


def _trunk(x, mem, W, bias_tabs, g_attn, g_qn, g_kvn, g_mem, g_mlp, g_final):
    B, S, D = x.shape
    T = B * S
    x2 = x.reshape(T, D)
    row = lambda g: g.reshape(1, -1).astype(F32)

    proj = _in_proj(x2, row(g_attn), W["w_in"])

    q_scale = (QK_NOPE + QK_ROPE) ** -0.5 * math.log2(math.e)
    qt, k, vt = _mla_proj(proj, _rope_tables(S), row(g_qn), row(g_kvn), W["wqt"], W["wk"], W["wvt"], S=S,
                          q_scale=q_scale)
    o_mla = _mla_flash(qt, k, vt, B=B, S=S)

    ods, lss = _dilated(proj, bias_tabs, B=B, S=S)

    kvm = _mem_kv(mem.reshape(B * N_MEM, D), row(g_mem), W["wmkv"])
    o_mem = _mem_attn(proj, kvm, S=S)

    merged = _merge(proj, o_mla, ods, lss, o_mem, W["wmla"], W["wdil"], W["wmem"])
    y = _out_mlp(x2, merged, W["wout"], row(g_mlp), W["wup"], W["wdn"], row(g_final))
    return y.reshape(B, S, D)


def kernel(x_prompt, x_sample, mem_prompt, mem_sample, rel_bias, g_attn, w_in, g_q_norm, w_uq, g_kv_norm, w_ukv,
           g_mem, w_mem_kv, w_b_mla, w_b_dil, w_b_mem, w_out, g_mlp, w_up, w_down, g_final):
    assert w_in.shape[0] == 1, "single layer"
    W = _prep_weights(w_in[0], w_uq[0], w_ukv[0], w_mem_kv[0], w_b_mla[0], w_b_dil[0], w_b_mem[0], w_out[0],
                      w_up[0], w_down[0])
    bias_tabs = _dil_bias_tables(rel_bias)
    args = (W, bias_tabs, g_attn[0], g_q_norm[0], g_kv_norm[0], g_mem[0], g_mlp[0], g_final)
    return (_trunk(x_prompt, mem_prompt, *args), _trunk(x_sample, mem_sample, *args))
```

```python
import functools
import math

import numpy as np
import jax
import jax.numpy as jnp
from jax import lax
from jax.experimental import pallas as pl
from jax.experimental.pallas import tpu as pltpu

F32 = jnp.float32
BF16 = jnp.bfloat16

D_MODEL = 2048
N_MEM = 256
MLA_HEADS = 12
Q_LORA = 512
KV_LORA = 512
QK_NOPE = 128
QK_ROPE = 64
V_HEAD = 128
ROPE_THETA = 10000.0
DIL_PAIRS = ((128, 1), (512, 4), (2048, 16))
DIL_GROUPS = 3
DIL_HEADS_PER_GROUP = 4
DIL_HEAD_DIM = 128
X_HEADS = 4
X_HEAD_DIM = 256
NUM_BUCKETS = 32
MAX_DISTANCE = 1024
D_FF = 4 * D_MODEL
N_BRANCH = 3
EPS = 1e-6
NEG_INF = -1e30

DIL_W = DIL_HEADS_PER_GROUP * DIL_HEAD_DIM
DIL_SIDE = 64
DIL_QB = 128
DIL_KB = DIL_QB + 2 * DIL_SIDE
PERM_BLOCK = 256
QK_PAD = 256
VT_ROWS = V_HEAD + 16

_PROJ_GATE = 0
_PROJ_XQ = _PROJ_GATE + N_BRANCH * D_MODEL
_PROJ_DIL = _PROJ_XQ + X_HEADS * X_HEAD_DIM
_PROJ_CQ = _PROJ_DIL + 3 * DIL_GROUPS * DIL_W
_PROJ_CKV = _PROJ_CQ + Q_LORA
_PROJ_KR = _PROJ_CKV + KV_LORA
PROJ_W = _PROJ_KR + 512

VMEM_LIMIT = 52 * 1024 * 1024


def _cparams(n_axes):
    return pltpu.CompilerParams(dimension_semantics=("arbitrary",) * n_axes, vmem_limit_bytes=VMEM_LIMIT)


def _rms(x32, g32):
    return x32 * lax.rsqrt(jnp.mean(x32 * x32, axis=-1, keepdims=True) + EPS) * g32


def _dot(a, b):
    return jnp.dot(a, b, preferred_element_type=F32)


def _dot_nt(a, b):
    return lax.dot_general(a, b, (((1,), (1,)), ((), ())), preferred_element_type=F32)


def _stream_perm(r):
    cr = PERM_BLOCK // r
    p = np.zeros((PERM_BLOCK, PERM_BLOCK), np.float32)
    for c in range(r):
        for a in range(cr):
            p[c * cr + a, r * a + c] = 1.0
    return p


def _in_proj_kernel(x_ref, g_ref, w_ref, p1_ref, p2_ref, o_ref, h_ref, *, tn):
    j = pl.program_id(1)

    @pl.when(j == 0)
    def _():
        h = _rms(x_ref[...], g_ref[...]).astype(BF16)
        h_ref[0] = h
        for v, p_ref in ((1, p1_ref), (2, p2_ref)):
            for b0 in range(0, h.shape[0], PERM_BLOCK):
                h_ref[v, b0:b0 + PERM_BLOCK, :] = _dot(p_ref[...], h[b0:b0 + PERM_BLOCK]).astype(BF16)

    half = tn // 2
    for k in range(2):
        lane0 = j * tn + k * half
        ver = (jnp.where(lane0 >= _PROJ_DIL + 3 * DIL_W, 1, 0) + jnp.where(lane0 >= _PROJ_DIL + 6 * DIL_W, 1, 0)
               - jnp.where(lane0 >= _PROJ_CQ, 2, 0))
        o_ref[:, k * half:(k + 1) * half] = _dot(h_ref[ver], w_ref[:, k * half:(k + 1) * half]).astype(BF16)


def _in_proj(x2, g, w, *, tm=1024, tn=1024):
    T, D = x2.shape
    assert tm % PERM_BLOCK == 0 and (3 * DIL_W) % (tn // 2) == 0 and _PROJ_DIL % (tn // 2) == 0
    perms = [jnp.asarray(_stream_perm(r), BF16) for _, r in DIL_PAIRS[1:]]
    const = lambda i, j: (0, 0)
    return pl.pallas_call(
        functools.partial(_in_proj_kernel, tn=tn),
        grid=(T // tm, PROJ_W // tn),
        in_specs=[
            pl.BlockSpec((tm, D), lambda i, j: (i, 0)),
            pl.BlockSpec((1, D), const),
            pl.BlockSpec((D, tn), lambda i, j: (0, j)),
            pl.BlockSpec((PERM_BLOCK, PERM_BLOCK), const),
            pl.BlockSpec((PERM_BLOCK, PERM_BLOCK), const),
        ],
        out_specs=pl.BlockSpec((tm, tn), lambda i, j: (i, j)),
        out_shape=jax.ShapeDtypeStruct((T, PROJ_W), BF16),
        scratch_shapes=[pltpu.VMEM((3, tm, D), BF16)],
        compiler_params=_cparams(2),
        name="in_proj",
    )(x2, g, w, *perms)


def _mla_proj_kernel(cq_ref, ckv_ref, kr_ref, cos_ref, sin_ref, cost_ref, sint_ref, gq_ref, gkv_ref,
                     wqt_ref, wk_ref, wvt_ref, qt_ref, k_ref, vt_ref, *, q_scale):
    half = QK_ROPE // 2
    hd = QK_NOPE + QK_ROPE
    nq_t = _rms(cq_ref[...].astype(F32), gq_ref[...]).T.astype(BF16)
    q_t = _dot(wqt_ref[...], nq_t)
    cos_t = cost_ref[...]
    sin_t = sint_ref[...]
    zeros = jnp.zeros((QK_PAD - hd, q_t.shape[1]), BF16)
    for h in range(MLA_HEADS):
        r0, o0 = h * hd, h * QK_PAD
        x1 = q_t[r0 + QK_NOPE:r0 + QK_NOPE + half]
        x2 = q_t[r0 + QK_NOPE + half:r0 + hd]
        qt_ref[o0:o0 + QK_NOPE, :] = (q_t[r0:r0 + QK_NOPE] * q_scale).astype(BF16)
        qt_ref[o0 + QK_NOPE:o0 + QK_NOPE + half, :] = ((x1 * cos_t - x2 * sin_t) * q_scale).astype(BF16)
        qt_ref[o0 + QK_NOPE + half:o0 + hd, :] = ((x2 * cos_t + x1 * sin_t) * q_scale).astype(BF16)
        qt_ref[o0 + hd:o0 + QK_PAD, :] = zeros

    nkv = _rms(ckv_ref[...].astype(F32), gkv_ref[...])
    v_t = _dot(wvt_ref[...], nkv.T.astype(BF16)).astype(BF16)
    extra = jnp.where(lax.broadcasted_iota(jnp.int32, (VT_ROWS - V_HEAD, v_t.shape[1]), 0) == 0, 1.0, 0.0).astype(BF16)
    for h in range(MLA_HEADS):
        vt_ref[h * VT_ROWS:h * VT_ROWS + V_HEAD, :] = v_t[h * V_HEAD:(h + 1) * V_HEAD]
        vt_ref[h * VT_ROWS + V_HEAD:(h + 1) * VT_ROWS, :] = extra
    kn = _dot(nkv.astype(BF16), wk_ref[...])
    kr = kr_ref[...].astype(F32)
    kpe = (kr[:, 0:128] * cos_ref[...] + kr[:, 128:256] * sin_ref[...]).astype(BF16)
    for h in range(MLA_HEADS):
        o0 = h * QK_PAD
        k_ref[:, o0:o0 + QK_NOPE] = kn[:, h * QK_NOPE:(h + 1) * QK_NOPE].astype(BF16)
        k_ref[:, o0 + QK_NOPE:o0 + QK_PAD] = kpe


def _mla_proj(proj, rope, gq, gkv, wqt, wk, wvt, *, S, q_scale, tm=512):
    T = proj.shape[0]
    nsb = S // tm
    HQ = MLA_HEADS * QK_PAD
    HV = MLA_HEADS * VT_ROWS
    cos_k, sin_k, cos_t, sin_t = rope
    const = lambda i: (0, 0)
    wspec = lambda w: pl.BlockSpec(w.shape, const, pipeline_mode=pl.Buffered(1))
    return pl.pallas_call(
        functools.partial(_mla_proj_kernel, q_scale=q_scale),
        grid=(T // tm,),
        in_specs=[
            pl.BlockSpec((tm, Q_LORA), lambda i: (i, _PROJ_CQ // 512)),
            pl.BlockSpec((tm, KV_LORA), lambda i: (i, _PROJ_CKV // 512)),
            pl.BlockSpec((tm, 512), lambda i: (i, _PROJ_KR // 512)),
            pl.BlockSpec((tm, 128), lambda i: (i % nsb, 0)),
            pl.BlockSpec((tm, 128), lambda i: (i % nsb, 0)),
            pl.BlockSpec((QK_ROPE // 2, tm), lambda i: (0, i % nsb)),
            pl.BlockSpec((QK_ROPE // 2, tm), lambda i: (0, i % nsb)),
            pl.BlockSpec((1, Q_LORA), const),
            pl.BlockSpec((1, KV_LORA), const),
            wspec(wqt), wspec(wk), wspec(wvt),
        ],
        out_specs=[
            pl.BlockSpec((HQ, tm), lambda i: (0, i)),
            pl.BlockSpec((tm, HQ), lambda i: (i, 0)),
            pl.BlockSpec((HV, tm), lambda i: (0, i)),
        ],
        out_shape=[
            jax.ShapeDtypeStruct((HQ, T), BF16),
            jax.ShapeDtypeStruct((T, HQ), BF16),
            jax.ShapeDtypeStruct((HV, T), BF16),
        ],
        compiler_params=_cparams(1),
        name="mla_proj",
    )(proj, proj, proj, cos_k, sin_k, cos_t, sin_t, gq, gkv, wqt, wk, wvt)


FLASH_SLAB = 32
FLASH_QK_ROWS = 256


FLASH_WINDOW = 100.0
FLASH_PV_AFTER = 4
FLASH_EXACT_ROWS = 512


def _zero_like_bits(v):
    u = pltpu.bitcast(v, jnp.uint32)
    return pltpu.bitcast(lax.shift_right_logical(lax.shift_right_logical(u, jnp.uint32(16)), jnp.uint32(16)), F32)


def _mla_flash_kernel(qt_ref, k_ref, vt_ref, o_ref, p0_ref, p1_ref, acc_ref, *, tq, tk, nk, nq):
    p_refs = (p0_ref, p1_ref)
    nc = nq * nk

    def chunk_pos(c):
        qi = c // nk
        return qi, c == qi * nk, pl.multiple_of((c - qi * nk) * tk, tk), pl.multiple_of(qi * tq, tq)

    def pv(c, slot):
        _, first, r0, _ = chunk_pos(c)
        acc = jnp.where(first, 0.0, acc_ref[...]) + _dot(vt_ref[:, pl.ds(r0, tk)], p_refs[slot][...])
        acc_ref[...] = acc
        return acc[0:1]

    def qk_exp(c, slot, st, pv_args=None):
        ref, rmax, done_ref, done_rmax = st
        _, first, r0, c0 = chunk_pos(c)
        done_ref = jnp.where(first, ref, done_ref)
        done_rmax = jnp.where(first, rmax, done_rmax)
        ref = jnp.where(first, rmax, ref)
        rmax = jnp.where(first, -jnp.inf, rmax)
        mx = None
        shift = ref
        for n, q0 in enumerate(range(0, tk, FLASH_QK_ROWS)):
            if n == FLASH_PV_AFTER and pv_args is not None:
                shift = ref + _zero_like_bits(pv(*pv_args))
            s = _dot(k_ref[pl.ds(r0 + q0, FLASH_QK_ROWS), :], qt_ref[:, pl.ds(c0, tq)])
            p_refs[slot][q0:q0 + FLASH_QK_ROWS, :] = jnp.exp2(s - shift).astype(BF16)
            part = jnp.max(s.reshape(FLASH_QK_ROWS // FLASH_SLAB, FLASH_SLAB, tq), axis=0)
            mx = part if mx is None else jnp.maximum(mx, part)
        rmax = jnp.maximum(rmax, jnp.max(mx, axis=0, keepdims=True))
        return ref, rmax, done_ref, done_rmax

    def exact_block(qi):
        qt = qt_ref[:, pl.ds(pl.multiple_of(qi * tq, tq), tq)]

        def step(j, carry):
            m, acc = carry
            r0 = pl.multiple_of(j * FLASH_EXACT_ROWS, FLASH_EXACT_ROWS)
            s = _dot(k_ref[pl.ds(r0, FLASH_EXACT_ROWS), :], qt)
            m_new = jnp.maximum(m, jnp.max(s, axis=0, keepdims=True))
            p = jnp.exp2(s - m_new).astype(BF16)
            return m_new, acc * jnp.exp2(m - m_new) + _dot(vt_ref[:, pl.ds(r0, FLASH_EXACT_ROWS)], p)

        init = (jnp.full((1, tq), -jnp.inf, F32), jnp.zeros((VT_ROWS, tq), F32))
        _, acc = lax.fori_loop(0, nk * tk // FLASH_EXACT_ROWS, step, init)
        acc_ref[...] = acc

    def finalize(c, ref_b, rmax_b):
        qi = c // nk
        in_window = jnp.max(jnp.abs(rmax_b - ref_b)) <= FLASH_WINDOW

        @pl.when(jnp.logical_not(in_window))
        def _():
            exact_block(qi)

        acc = acc_ref[...]
        o = acc[:V_HEAD] * (1.0 / acc[V_HEAD:V_HEAD + 1])
        o_ref[pl.ds(pl.multiple_of(qi * tq, tq), tq), :] = o.T.astype(BF16)

    mx = None
    for q0 in range(0, tk, FLASH_QK_ROWS):
        s = _dot(k_ref[q0:q0 + FLASH_QK_ROWS, :], qt_ref[:, 0:tq])
        part = jnp.max(s.reshape(FLASH_QK_ROWS // FLASH_SLAB, FLASH_SLAB, tq), axis=0)
        mx = part if mx is None else jnp.maximum(mx, part)
    first_max = jnp.max(mx, axis=0, keepdims=True)
    ninf = jnp.full((1, tq), -jnp.inf, F32)
    st = qk_exp(0, 0, (first_max, first_max, ninf, ninf))

    def chunks_of_block(i, st, n):
        for j in range(n):
            c = i * nk + j
            st = qk_exp(c + 1, (j + 1) % 2, st, (c, j % 2))
        return st

    def block(i, st):
        st = chunks_of_block(i, st, nk)
        finalize(i * nk + nk - 1, st[2], st[3])
        return st

    st = lax.fori_loop(0, nq - 1, block, st)
    st = chunks_of_block(nq - 1, st, nk - 1)
    pv(nc - 1, (nk - 1) % 2)
    finalize(nc - 1, st[0], st[1])


def _mla_flash(qt, k, vt, *, B, S, tq=512, tk=2048):
    T = B * S
    nq, nk = S // tq, S // tk
    assert nk % 2 == 0 and tk // FLASH_QK_ROWS > FLASH_PV_AFTER
    return pl.pallas_call(
        functools.partial(_mla_flash_kernel, tq=tq, tk=tk, nk=nk, nq=nq),
        grid=(B, MLA_HEADS),
        in_specs=[
            pl.BlockSpec((QK_PAD, S), lambda b, h: (h, b)),
            pl.BlockSpec((S, QK_PAD), lambda b, h: (b, h)),
            pl.BlockSpec((VT_ROWS, S), lambda b, h: (h, b)),
        ],
        out_specs=pl.BlockSpec((S, V_HEAD), lambda b, h: (b, h)),
        out_shape=jax.ShapeDtypeStruct((T, MLA_HEADS * V_HEAD), BF16),
        scratch_shapes=[pltpu.VMEM((tk, tq), BF16), pltpu.VMEM((tk, tq), BF16), pltpu.VMEM((VT_ROWS, tq), F32)],
        compiler_params=_cparams(2),
        name="mla_flash",
    )(qt, k, vt)


def _dil_kernel(q_ref, k_ref, v_ref, bias_ref, o_ref, lse_ref, *, L, ta, cr):
    scale = DIL_HEAD_DIM ** -0.5
    lane = lax.broadcasted_iota(jnp.int32, (DIL_QB, 128), 1)
    base = pl.program_id(2) * ta

    def rows_of(ref, start, n, cs):
        lo = start // cr
        idx = slice(lo, lo + n // cr) if isinstance(start, int) else pl.ds(lo, n // cr)
        return ref[0, idx, 0, :, cs].reshape(n, DIL_HEAD_DIM)

    for t in range(ta // DIL_QB):
        a0 = base + t * DIL_QB
        k0 = jnp.clip(a0 - DIL_SIDE, 0, L - DIL_KB)
        var = (a0 - k0) // DIL_SIDE
        chunks = slice(t * DIL_QB // cr, (t + 1) * DIL_QB // cr)
        lses = []
        for h in range(DIL_HEADS_PER_GROUP):
            cs = slice(h * DIL_HEAD_DIM, (h + 1) * DIL_HEAD_DIM)
            qh = rows_of(q_ref, t * DIL_QB, DIL_QB, cs)
            kh = rows_of(k_ref, k0, DIL_KB, cs)
            vh = rows_of(v_ref, k0, DIL_KB, cs)
            s = _dot_nt(qh, kh) * scale + bias_ref[var, h]
            m = jnp.max(s, axis=-1, keepdims=True)
            p = jnp.exp(s - m)
            l = jnp.sum(p, axis=-1, keepdims=True)
            o = (_dot(p.astype(BF16), vh) / l).astype(BF16)
            o_ref[0, chunks, 0, :, cs] = o.reshape(DIL_QB // cr, cr, DIL_HEAD_DIM)
            lses.append(m + jnp.log(l))
        packed = jnp.where(lane < 32, lses[0], jnp.where(lane < 64, lses[1], jnp.where(lane < 96, lses[2], lses[3])))
        lse_ref[0, chunks, 0, :, :] = packed.reshape(DIL_QB // cr, cr, 128)


def _dilated_group(proj, bias, *, B, S, r, col, name):
    cr = min(PERM_BLOCK // r, DIL_SIDE)
    assert r == 1 or r * cr == PERM_BLOCK
    L = S // r
    per_stream = L // cr
    ta = min(L, 512)
    x5 = proj.reshape(B, per_stream, r, cr, PROJ_W)
    return pl.pallas_call(
        functools.partial(_dil_kernel, L=L, ta=ta, cr=cr),
        grid=(B, r, L // ta),
        in_specs=[
            pl.BlockSpec((1, ta // cr, 1, cr, DIL_W), lambda b, c, a: (b, a, c, 0, col)),
            pl.BlockSpec((1, per_stream, 1, cr, DIL_W), lambda b, c, a: (b, 0, c, 0, col + 1)),
            pl.BlockSpec((1, per_stream, 1, cr, DIL_W), lambda b, c, a: (b, 0, c, 0, col + 2)),
            pl.BlockSpec(bias.shape, lambda b, c, a: (0, 0, 0, 0)),
        ],
        out_specs=[
            pl.BlockSpec((1, ta // cr, 1, cr, DIL_W), lambda b, c, a: (b, a, c, 0, 0)),
            pl.BlockSpec((1, ta // cr, 1, cr, 128), lambda b, c, a: (b, a, c, 0, 0)),
        ],
        out_shape=[
            jax.ShapeDtypeStruct((B, per_stream, r, cr, DIL_W), BF16),
            jax.ShapeDtypeStruct((B, per_stream, r, cr, 128), F32),
        ],
        compiler_params=_cparams(3),
        name=name,
    )(x5, x5, x5, bias)


def _dilated(proj, bias_tabs, *, B, S):
    T = B * S
    ods, lss = [], []
    for g, (_, r) in enumerate(DIL_PAIRS):
        od, ls = _dilated_group(proj, bias_tabs[g], B=B, S=S, r=r, col=(_PROJ_DIL + g * 3 * DIL_W) // DIL_W,
                                name=f"dilated_g{g}")
        ods.append(od.reshape(T, DIL_W))
        lss.append(ls.reshape(T, 128))
    return ods, lss


def _t5_bucket(rel):
    nb = NUM_BUCKETS // 2
    ret = (rel > 0).astype(np.int32) * nb
    n = np.abs(rel)
    max_exact = nb // 2
    large = max_exact + (np.log(np.maximum(n, 1) / max_exact) / np.log(MAX_DISTANCE / max_exact)
                         * (nb - max_exact)).astype(np.int32)
    large = np.minimum(large, nb - 1)
    return (ret + np.where(n < max_exact, n, large)).astype(np.int32)


def _dil_bias_tables(rel_bias):
    qq = np.arange(DIL_QB)[:, None]
    kk = np.arange(DIL_KB)[None, :]
    ext = jnp.concatenate([rel_bias.astype(F32), jnp.full((1, rel_bias.shape[1]), NEG_INF, F32)], axis=0)
    tables = []
    for g, (_, r) in enumerate(DIL_PAIRS):
        idx = []
        for var in range(3):
            j = kk - qq - var * DIL_SIDE
            bucket = _t5_bucket(j * r)
            idx.append(np.where(np.abs(j) <= DIL_SIDE, bucket, NUM_BUCKETS))
        onehot = (np.stack(idx)[..., None] == np.arange(NUM_BUCKETS + 1)).astype(np.float32)
        heads = ext[:, g * DIL_HEADS_PER_GROUP:(g + 1) * DIL_HEADS_PER_GROUP]
        tables.append(jnp.einsum("vqkn,nh->vhqk", onehot, heads, precision=lax.Precision.HIGHEST))
    return tables


def _mem_kv_kernel(m_ref, g_ref, w_ref, o_ref):
    o_ref[...] = _dot(_rms(m_ref[...], g_ref[...]).astype(BF16), w_ref[...]).astype(BF16)


def _mem_kv(mem2, g, w):
    R, D = mem2.shape
    N = w.shape[1]
    return pl.pallas_call(
        _mem_kv_kernel,
        grid=(R // N_MEM,),
        in_specs=[
            pl.BlockSpec((N_MEM, D), lambda i: (i, 0)),
            pl.BlockSpec((1, D), lambda i: (0, 0)),
            pl.BlockSpec((D, N), lambda i: (0, 0)),
        ],
        out_specs=pl.BlockSpec((N_MEM, N), lambda i: (i, 0)),
        out_shape=jax.ShapeDtypeStruct((R, N), BF16),
        compiler_params=_cparams(1),
        name="mem_kv",
    )(mem2, g, w)


def _mem_attn_kernel(q_ref, kv_ref, o_ref):
    scale = X_HEAD_DIM ** -0.5
    hv = X_HEADS * X_HEAD_DIM
    for h in range(X_HEADS):
        cs = slice(h * X_HEAD_DIM, (h + 1) * X_HEAD_DIM)
        s = _dot_nt(q_ref[:, cs], kv_ref[:, cs]) * scale
        m = jnp.max(s, axis=-1, keepdims=True)
        p = jnp.exp(s - m)
        l = jnp.sum(p, axis=-1, keepdims=True)
        o = _dot(p.astype(BF16), kv_ref[:, hv + h * X_HEAD_DIM:hv + (h + 1) * X_HEAD_DIM]) / l
        o_ref[:, cs] = o.astype(BF16)


def _mem_attn(proj, kvm, *, S, tm=512):
    T = proj.shape[0]
    W = X_HEADS * X_HEAD_DIM
    nsb = S // tm
    return pl.pallas_call(
        _mem_attn_kernel,
        grid=(T // tm,),
        in_specs=[
            pl.BlockSpec((tm, W), lambda i: (i, _PROJ_XQ // W)),
            pl.BlockSpec((N_MEM, 2 * W), lambda i: (i // nsb, 0)),
        ],
        out_specs=pl.BlockSpec((tm, W), lambda i: (i, 0)),
        out_shape=jax.ShapeDtypeStruct((T, W), BF16),
        compiler_params=_cparams(1),
        name="mem_attn",
    )(proj, kvm)


def _token_order(pt_ref, od_ref, ls_ref):
    ods, lss = [], []
    for b0 in range(0, od_ref.shape[0], PERM_BLOCK):
        rows = slice(b0, b0 + PERM_BLOCK)
        ods.append(_dot(pt_ref[...], od_ref[rows, :]))
        ls = ls_ref[rows, :]
        hi = ls.astype(BF16)
        rest = ls - hi.astype(F32)
        mid = rest.astype(BF16)
        lo = (rest - mid.astype(F32)).astype(BF16)
        parts = _dot(pt_ref[...], jnp.concatenate([hi, mid, lo], axis=-1))
        lss.append(parts[:, 0:128] + parts[:, 128:256] + parts[:, 256:384])
    return jnp.concatenate(ods, axis=0), jnp.concatenate(lss, axis=0)


def _merge_kernel(gate_ref, omla_ref, od0_ref, od1_ref, od2_ref, ls0_ref, ls1_ref, ls2_ref, omem_ref,
                  wmla_ref, wdil_ref, wmem_ref, pt1_ref, pt2_ref, o_ref):
    od1, ls1 = _token_order(pt1_ref, od1_ref, ls1_ref)
    od2, ls2 = _token_order(pt2_ref, od2_ref, ls2_ref)
    ls = [ls0_ref[...], ls1, ls2]
    mx = jnp.maximum(jnp.maximum(ls[0], ls[1]), ls[2])
    es = [jnp.exp(x - mx) for x in ls]
    inv = 1.0 / (es[0] + es[1] + es[2])
    od = [od0_ref[...].astype(F32), od1, od2]
    parts = []
    for h in range(DIL_HEADS_PER_GROUP):
        cs = slice(h * DIL_HEAD_DIM, (h + 1) * DIL_HEAD_DIM)
        acc = None
        for gi in range(DIL_GROUPS):
            a = (es[gi] * inv)[:, h * 32:h * 32 + 1]
            term = a * od[gi][:, cs]
            acc = term if acc is None else acc + term
        parts.append(acc.astype(BF16))
    o_dil = jnp.concatenate(parts, axis=-1)

    def gate(b):
        return jax.nn.sigmoid(gate_ref[:, b * D_MODEL:(b + 1) * D_MODEL].astype(F32))

    merged = gate(0) * _dot(omla_ref[...], wmla_ref[...])
    merged += gate(1) * _dot(o_dil, wdil_ref[...])
    merged += gate(2) * _dot(omem_ref[...], wmem_ref[...])
    o_ref[...] = merged.astype(BF16)


def _merge(proj, o_mla, ods, lss, o_mem, wmla, wdil, wmem, *, tm=512):
    T = proj.shape[0]
    assert tm % PERM_BLOCK == 0
    row = lambda i: (i, 0)
    const = lambda i: (0, 0)
    wspec = lambda w: pl.BlockSpec(w.shape, const, pipeline_mode=pl.Buffered(1))
    perms_t = [jnp.asarray(_stream_perm(r).T, BF16) for _, r in DIL_PAIRS[1:]]
    return pl.pallas_call(
        _merge_kernel,
        grid=(T // tm,),
        in_specs=[
            pl.BlockSpec((tm, N_BRANCH * D_MODEL), row),
            pl.BlockSpec((tm, o_mla.shape[1]), row),
            pl.BlockSpec((tm, DIL_W), row), pl.BlockSpec((tm, DIL_W), row), pl.BlockSpec((tm, DIL_W), row),
            pl.BlockSpec((tm, 128), row), pl.BlockSpec((tm, 128), row), pl.BlockSpec((tm, 128), row),
            pl.BlockSpec((tm, o_mem.shape[1]), row),
            wspec(wmla), wspec(wdil), wspec(wmem),
            pl.BlockSpec((PERM_BLOCK, PERM_BLOCK), const), pl.BlockSpec((PERM_BLOCK, PERM_BLOCK), const),
        ],
        out_specs=pl.BlockSpec((tm, D_MODEL), row),
        out_shape=jax.ShapeDtypeStruct((T, D_MODEL), BF16),
        compiler_params=_cparams(1),
        name="merge",
    )(proj, o_mla, *ods, *lss, o_mem, wmla, wdil, wmem, *perms_t)


def _out_mlp_kernel(x_ref, mg_ref, wout_ref, gm_ref, wup_ref, wdn_ref, gf_ref, o_ref, h_ref):
    j = pl.program_id(1)

    @pl.when(j == 0)
    def _():
        x1 = x_ref[...] + _dot(mg_ref[...], wout_ref[...])
        o_ref[...] = x1
        h_ref[...] = _rms(x1, gm_ref[...]).astype(BF16)

    u = jnp.maximum(_dot(h_ref[...], wup_ref[...]), 0.0)
    o_ref[...] += _dot((u * u).astype(BF16), wdn_ref[...])

    @pl.when(j == pl.num_programs(1) - 1)
    def _():
        o_ref[...] = _rms(o_ref[...], gf_ref[...])


def _out_mlp(x2, merged, wout, gm, wup, wdn, gf, *, tm=512, tf=1024):
    T, D = x2.shape
    row = lambda i, j: (i, 0)
    const = lambda i, j: (0, 0)
    return pl.pallas_call(
        _out_mlp_kernel,
        grid=(T // tm, D_FF // tf),
        in_specs=[
            pl.BlockSpec((tm, D), row),
            pl.BlockSpec((tm, D), row),
            pl.BlockSpec((D, D), const, pipeline_mode=pl.Buffered(1)),
            pl.BlockSpec((1, D), const),
            pl.BlockSpec((D, tf), lambda i, j: (0, j)),
            pl.BlockSpec((tf, D), lambda i, j: (j, 0)),
            pl.BlockSpec((1, D), const),
        ],
        out_specs=pl.BlockSpec((tm, D), row),
        out_shape=jax.ShapeDtypeStruct((T, D), F32),
        scratch_shapes=[pltpu.VMEM((tm, D), BF16)],
        compiler_params=_cparams(2),
        name="out_mlp",
    )(x2, merged, wout, gm, wup, wdn, gf)


def _rot_half_cols(w):
    half = QK_ROPE // 2
    return jnp.concatenate([-w[..., half:], w[..., :half]], axis=-1)


def _prep_weights(w_in, w_uq, w_ukv, w_mem_kv, w_b_mla, w_b_dil, w_b_mem, w_out, w_up, w_down):
    D = w_in.shape[0]
    w_in = w_in.astype(BF16)
    o = 0
    cq = w_in[:, o:o + Q_LORA]; o += Q_LORA
    ckv = w_in[:, o:o + KV_LORA]; o += KV_LORA
    kr = w_in[:, o:o + QK_ROPE]; o += QK_ROPE
    dil = w_in[:, o:o + 3 * DIL_GROUPS * DIL_W]; o += 3 * DIL_GROUPS * DIL_W
    xq = w_in[:, o:o + X_HEADS * X_HEAD_DIM]; o += X_HEADS * X_HEAD_DIM
    gate = w_in[:, o:o + N_BRANCH * D_MODEL]
    dil = dil.reshape(D, 3, DIL_GROUPS, DIL_W).transpose(0, 2, 1, 3).reshape(D, 3 * DIL_GROUPS * DIL_W)
    z64 = jnp.zeros((D, 64), BF16)
    w_in_p = jnp.concatenate(
        [gate, xq, dil, cq, ckv, kr, z64, _rot_half_cols(kr), z64, jnp.zeros((D, 256), BF16)], axis=1)

    wkv = w_ukv.reshape(KV_LORA, MLA_HEADS, QK_NOPE + V_HEAD)
    wk = wkv[..., :QK_NOPE].reshape(KV_LORA, MLA_HEADS * QK_NOPE).astype(BF16)
    wvt = wkv[..., QK_NOPE:].reshape(KV_LORA, MLA_HEADS * V_HEAD).T.astype(BF16)
    return dict(w_in=w_in_p, wqt=w_uq.T.astype(BF16), wk=wk, wvt=wvt, wmkv=w_mem_kv.astype(BF16),
                wmla=w_b_mla.astype(BF16), wdil=w_b_dil.astype(BF16), wmem=w_b_mem.astype(BF16),
                wout=w_out.astype(BF16), wup=w_up.astype(BF16), wdn=w_down.astype(BF16))


def _rope_tables(S):
    half = QK_ROPE // 2
    inv = 1.0 / (ROPE_THETA ** (jnp.arange(half, dtype=F32) / half))
    ang = jnp.arange(S).astype(F32)[:, None] * inv[None, :]
    z = jnp.zeros((S, 64), F32)
    cos, sin = jnp.cos(ang), jnp.sin(ang)
    return jnp.concatenate([cos, cos, z], axis=1), jnp.concatenate([sin, sin, z], axis=1), cos.T, sin.T


def _trunk(x, mem, W, bias_tabs, g_attn, g_qn, g_kvn, g_mem, g_mlp, g_final):
    B, S, D = x.shape
    T = B * S
    x2 = x.reshape(T, D)
    row = lambda g: g.reshape(1, -1).astype(F32)

    proj = _in_proj(x2, row(g_attn), W["w_in"])

    q_scale = (QK_NOPE + QK_ROPE) ** -0.5 * math.log2(math.e)
    qt, k, vt = _mla_proj(proj, _rope_tables(S), row(g_qn), row(g_kvn), W["wqt"], W["wk"], W["wvt"], S=S,
                          q_scale=q_scale)
    o_mla = _mla_flash(qt, k, vt, B=B, S=S)

    ods, lss = _dilated(proj, bias_tabs, B=B, S=S)

    kvm = _mem_kv(mem.reshape(B * N_MEM, D), row(g_mem), W["wmkv"])
    o_mem = _mem_attn(proj, kvm, S=S)

    merged = _merge(proj, o_mla, ods, lss, o_mem, W["wmla"], W["wdil"], W["wmem"])
    y = _out_mlp(x2, merged, W["wout"], row(g_mlp), W["wup"], W["wdn"], row(g_final))
    return y.reshape(B, S, D)


def kernel(x_prompt, x_sample, mem_prompt, mem_sample, rel_bias, g_attn, w_in, g_q_norm, w_uq, g_kv_norm, w_ukv,
           g_mem, w_mem_kv, w_b_mla, w_b_dil, w_b_mem, w_out, g_mlp, w_up, w_down, g_final):
    assert w_in.shape[0] == 1, "single layer"
    W = _prep_weights(w_in[0], w_uq[0], w_ukv[0], w_mem_kv[0], w_b_mla[0], w_b_dil[0], w_b_mem[0], w_out[0],
                      w_up[0], w_down[0])
    bias_tabs = _dil_bias_tables(rel_bias)
    args = (W, bias_tabs, g_attn[0], g_q_norm[0], g_kv_norm[0], g_mem[0], g_mlp[0], g_final)
    return (_trunk(x_prompt, mem_prompt, *args), _trunk(x_sample, mem_sample, *args))
```

```python
import functools
import math

import numpy as np
import jax
import jax.numpy as jnp
from jax import lax
from jax.experimental import pallas as pl
from jax.experimental.pallas import tpu as pltpu

F32 = jnp.float32
BF16 = jnp.bfloat16

D_MODEL = 2048
N_MEM = 256
MLA_HEADS = 12
Q_LORA = 512
KV_LORA = 512
QK_NOPE = 128
QK_ROPE = 64
V_HEAD = 128
ROPE_THETA = 10000.0
DIL_PAIRS = ((128, 1), (512, 4), (2048, 16))
DIL_GROUPS = 3
DIL_HEADS_PER_GROUP = 4
DIL_HEAD_DIM = 128
X_HEADS = 4
X_HEAD_DIM = 256
NUM_BUCKETS = 32
MAX_DISTANCE = 1024
D_FF = 4 * D_MODEL
N_BRANCH = 3
EPS = 1e-6
NEG_INF = -1e30
LOG2E = math.log2(math.e)

DIL_W = DIL_HEADS_PER_GROUP * DIL_HEAD_DIM
DIL_SIDE = 64
DIL_QB = 128
DIL_KB = DIL_QB + 2 * DIL_SIDE
PERM_BLOCK = 256
QK_PAD = 256
VT_ROWS = V_HEAD + 16

_PROJ_GATE = 0
_PROJ_XQ = _PROJ_GATE + N_BRANCH * D_MODEL
_PROJ_DIL = _PROJ_XQ + X_HEADS * X_HEAD_DIM
_PROJ_CQ = _PROJ_DIL + 3 * DIL_GROUPS * DIL_W
_PROJ_CKV = _PROJ_CQ + Q_LORA
_PROJ_KR = _PROJ_CKV + KV_LORA
PROJ_W = _PROJ_KR + 512

VMEM_LIMIT = 52 * 1024 * 1024


def _cparams(n_axes):
    return pltpu.CompilerParams(dimension_semantics=("arbitrary",) * n_axes, vmem_limit_bytes=VMEM_LIMIT)


def _rms(x32, g32):
    return x32 * lax.rsqrt(jnp.mean(x32 * x32, axis=-1, keepdims=True) + EPS) * g32


def _dot(a, b):
    return jnp.dot(a, b, preferred_element_type=F32)


def _dot_nt(a, b):
    return lax.dot_general(a, b, (((1,), (1,)), ((), ())), preferred_element_type=F32)


def _stream_perm(r):
    cr = PERM_BLOCK // r
    p = np.zeros((PERM_BLOCK, PERM_BLOCK), np.float32)
    for c in range(r):
        for a in range(cr):
            p[c * cr + a, r * a + c] = 1.0
    return p


def _in_proj_kernel(x_ref, g_ref, w_ref, p1_ref, p2_ref, o_ref, h_ref, *, tn):
    j = pl.program_id(1)

    @pl.when(j == 0)
    def _():
        for b0 in range(0, x_ref.shape[0], PERM_BLOCK):
            rows = slice(b0, b0 + PERM_BLOCK)
            h = _rms(x_ref[rows, :], g_ref[...]).astype(BF16)
            h_ref[0, rows, :] = h
            h_ref[1, rows, :] = _dot(p1_ref[...], h).astype(BF16)
            h_ref[2, rows, :] = _dot(p2_ref[...], h).astype(BF16)

    half = tn // 2
    for k in range(2):
        lane0 = j * tn + k * half
        ver = (jnp.where(lane0 >= _PROJ_DIL + 3 * DIL_W, 1, 0) + jnp.where(lane0 >= _PROJ_DIL + 6 * DIL_W, 1, 0)
               - jnp.where(lane0 >= _PROJ_CQ, 2, 0))
        o_ref[:, k * half:(k + 1) * half] = _dot(h_ref[ver], w_ref[:, k * half:(k + 1) * half]).astype(BF16)


def _in_proj(x2, g, w, *, tm=1024, tn=1024):
    T, D = x2.shape
    assert tm % PERM_BLOCK == 0 and (3 * DIL_W) % (tn // 2) == 0 and _PROJ_DIL % (tn // 2) == 0
    perms = [jnp.asarray(_stream_perm(r), BF16) for _, r in DIL_PAIRS[1:]]
    const = lambda i, j: (0, 0)
    return pl.pallas_call(
        functools.partial(_in_proj_kernel, tn=tn),
        grid=(T // tm, PROJ_W // tn),
        in_specs=[
            pl.BlockSpec((tm, D), lambda i, j: (i, 0)),
            pl.BlockSpec((1, D), const),
            pl.BlockSpec((D, tn), lambda i, j: (0, j)),
            pl.BlockSpec((PERM_BLOCK, PERM_BLOCK), const),
            pl.BlockSpec((PERM_BLOCK, PERM_BLOCK), const),
        ],
        out_specs=pl.BlockSpec((tm, tn), lambda i, j: (i, j)),
        out_shape=jax.ShapeDtypeStruct((T, PROJ_W), BF16),
        scratch_shapes=[pltpu.VMEM((3, tm, D), BF16)],
        compiler_params=_cparams(2),
        name="in_proj",
    )(x2, g, w, *perms)


def _mla_proj_kernel(cq_ref, ckv_ref, kr_ref, cos_ref, sin_ref, cost_ref, sint_ref, gq_ref, gkv_ref,
                     wqt_ref, wk_ref, wvt_ref, qt_ref, k_ref, vt_ref, *, q_scale):
    half = QK_ROPE // 2
    hd = QK_NOPE + QK_ROPE
    nq_t = _rms(cq_ref[...].astype(F32), gq_ref[...]).T.astype(BF16)
    q_t = _dot(wqt_ref[...], nq_t)
    cos_t = cost_ref[...]
    sin_t = sint_ref[...]
    zeros = jnp.zeros((QK_PAD - hd, q_t.shape[1]), BF16)
    for h in range(MLA_HEADS):
        r0, o0 = h * hd, h * QK_PAD
        x1 = q_t[r0 + QK_NOPE:r0 + QK_NOPE + half]
        x2 = q_t[r0 + QK_NOPE + half:r0 + hd]
        qt_ref[o0:o0 + QK_NOPE, :] = (q_t[r0:r0 + QK_NOPE] * q_scale).astype(BF16)
        qt_ref[o0 + QK_NOPE:o0 + QK_NOPE + half, :] = ((x1 * cos_t - x2 * sin_t) * q_scale).astype(BF16)
        qt_ref[o0 + QK_NOPE + half:o0 + hd, :] = ((x2 * cos_t + x1 * sin_t) * q_scale).astype(BF16)
        qt_ref[o0 + hd:o0 + QK_PAD, :] = zeros

    nkv = _rms(ckv_ref[...].astype(F32), gkv_ref[...])
    v_t = _dot(wvt_ref[...], nkv.T.astype(BF16)).astype(BF16)
    extra = jnp.where(lax.broadcasted_iota(jnp.int32, (VT_ROWS - V_HEAD, v_t.shape[1]), 0) == 0, 1.0, 0.0).astype(BF16)
    for h in range(MLA_HEADS):
        vt_ref[h * VT_ROWS:h * VT_ROWS + V_HEAD, :] = v_t[h * V_HEAD:(h + 1) * V_HEAD]
        vt_ref[h * VT_ROWS + V_HEAD:(h + 1) * VT_ROWS, :] = extra
    kn = _dot(nkv.astype(BF16), wk_ref[...])
    kr = kr_ref[...].astype(F32)
    kpe = (kr[:, 0:128] * cos_ref[...] + kr[:, 128:256] * sin_ref[...]).astype(BF16)
    for h in range(MLA_HEADS):
        o0 = h * QK_PAD
        k_ref[:, o0:o0 + QK_NOPE] = kn[:, h * QK_NOPE:(h + 1) * QK_NOPE].astype(BF16)
        k_ref[:, o0 + QK_NOPE:o0 + QK_PAD] = kpe


def _mla_proj(proj, rope, gq, gkv, wqt, wk, wvt, *, S, q_scale, tm=512):
    T = proj.shape[0]
    nsb = S // tm
    HQ = MLA_HEADS * QK_PAD
    HV = MLA_HEADS * VT_ROWS
    cos_k, sin_k, cos_t, sin_t = rope
    const = lambda i: (0, 0)
    wspec = lambda w: pl.BlockSpec(w.shape, const, pipeline_mode=pl.Buffered(1))
    return pl.pallas_call(
        functools.partial(_mla_proj_kernel, q_scale=q_scale),
        grid=(T // tm,),
        in_specs=[
            pl.BlockSpec((tm, Q_LORA), lambda i: (i, _PROJ_CQ // 512)),
            pl.BlockSpec((tm, KV_LORA), lambda i: (i, _PROJ_CKV // 512)),
            pl.BlockSpec((tm, 512), lambda i: (i, _PROJ_KR // 512)),
            pl.BlockSpec((tm, 128), lambda i: (i % nsb, 0)),
            pl.BlockSpec((tm, 128), lambda i: (i % nsb, 0)),
            pl.BlockSpec((QK_ROPE // 2, tm), lambda i: (0, i % nsb)),
            pl.BlockSpec((QK_ROPE // 2, tm), lambda i: (0, i % nsb)),
            pl.BlockSpec((1, Q_LORA), const),
            pl.BlockSpec((1, KV_LORA), const),
            wspec(wqt), wspec(wk), wspec(wvt),
        ],
        out_specs=[
            pl.BlockSpec((HQ, tm), lambda i: (0, i)),
            pl.BlockSpec((tm, HQ), lambda i: (i, 0)),
            pl.BlockSpec((HV, tm), lambda i: (0, i)),
        ],
        out_shape=[
            jax.ShapeDtypeStruct((HQ, T), BF16),
            jax.ShapeDtypeStruct((T, HQ), BF16),
            jax.ShapeDtypeStruct((HV, T), BF16),
        ],
        compiler_params=_cparams(1),
        name="mla_proj",
    )(proj, proj, proj, cos_k, sin_k, cos_t, sin_t, gq, gkv, wqt, wk, wvt)


FLASH_SLAB = 32
FLASH_QK_ROWS = 256


FLASH_WINDOW = 100.0
FLASH_PV_AFTER = 4
FLASH_EXACT_ROWS = 512


def _zero_like_bits(v):
    u = pltpu.bitcast(v, jnp.uint32)
    return pltpu.bitcast(lax.shift_right_logical(lax.shift_right_logical(u, jnp.uint32(16)), jnp.uint32(16)), F32)


def _mla_flash_kernel(qt_ref, k_ref, vt_ref, o_ref, p0_ref, p1_ref, acc_ref, *, tq, tk, nk, nq):
    p_refs = (p0_ref, p1_ref)
    nc = nq * nk

    def chunk_pos(c):
        qi = c // nk
        return qi, c == qi * nk, pl.multiple_of((c - qi * nk) * tk, tk), pl.multiple_of(qi * tq, tq)

    def pv(c, slot):
        _, first, r0, _ = chunk_pos(c)
        acc = jnp.where(first, 0.0, acc_ref[...]) + _dot(vt_ref[:, pl.ds(r0, tk)], p_refs[slot][...])
        acc_ref[...] = acc
        return acc[0:1]

    def qk_exp(c, slot, st, pv_args=None):
        ref, rmax, done_ref, done_rmax = st
        _, first, r0, c0 = chunk_pos(c)
        done_ref = jnp.where(first, ref, done_ref)
        done_rmax = jnp.where(first, rmax, done_rmax)
        ref = jnp.where(first, rmax, ref)
        rmax = jnp.where(first, -jnp.inf, rmax)
        mx = None
        shift = ref
        for n, q0 in enumerate(range(0, tk, FLASH_QK_ROWS)):
            if n == FLASH_PV_AFTER and pv_args is not None:
                shift = ref + _zero_like_bits(pv(*pv_args))
            s = _dot(k_ref[pl.ds(r0 + q0, FLASH_QK_ROWS), :], qt_ref[:, pl.ds(c0, tq)])
            p_refs[slot][q0:q0 + FLASH_QK_ROWS, :] = jnp.exp2(s - shift).astype(BF16)
            part = jnp.max(s.reshape(FLASH_QK_ROWS // FLASH_SLAB, FLASH_SLAB, tq), axis=0)
            mx = part if mx is None else jnp.maximum(mx, part)
        rmax = jnp.maximum(rmax, jnp.max(mx, axis=0, keepdims=True))
        return ref, rmax, done_ref, done_rmax

    def exact_block(qi):
        qt = qt_ref[:, pl.ds(pl.multiple_of(qi * tq, tq), tq)]

        def step(j, carry):
            m, acc = carry
            r0 = pl.multiple_of(j * FLASH_EXACT_ROWS, FLASH_EXACT_ROWS)
            s = _dot(k_ref[pl.ds(r0, FLASH_EXACT_ROWS), :], qt)
            m_new = jnp.maximum(m, jnp.max(s, axis=0, keepdims=True))
            p = jnp.exp2(s - m_new).astype(BF16)
            return m_new, acc * jnp.exp2(m - m_new) + _dot(vt_ref[:, pl.ds(r0, FLASH_EXACT_ROWS)], p)

        init = (jnp.full((1, tq), -jnp.inf, F32), jnp.zeros((VT_ROWS, tq), F32))
        _, acc = lax.fori_loop(0, nk * tk // FLASH_EXACT_ROWS, step, init)
        acc_ref[...] = acc

    def finalize(c, ref_b, rmax_b):
        qi = c // nk
        in_window = jnp.max(jnp.abs(rmax_b - ref_b)) <= FLASH_WINDOW

        @pl.when(jnp.logical_not(in_window))
        def _():
            exact_block(qi)

        acc = acc_ref[...]
        o = acc[:V_HEAD] * (1.0 / acc[V_HEAD:V_HEAD + 1])
        o_ref[pl.ds(pl.multiple_of(qi * tq, tq), tq), :] = o.T.astype(BF16)

    mx = None
    for q0 in range(0, tk, FLASH_QK_ROWS):
        s = _dot(k_ref[q0:q0 + FLASH_QK_ROWS, :], qt_ref[:, 0:tq])
        part = jnp.max(s.reshape(FLASH_QK_ROWS // FLASH_SLAB, FLASH_SLAB, tq), axis=0)
        mx = part if mx is None else jnp.maximum(mx, part)
    first_max = jnp.max(mx, axis=0, keepdims=True)
    ninf = jnp.full((1, tq), -jnp.inf, F32)
    st = qk_exp(0, 0, (first_max, first_max, ninf, ninf))

    def chunks_of_block(i, st, n):
        for j in range(n):
            c = i * nk + j
            st = qk_exp(c + 1, (j + 1) % 2, st, (c, j % 2))
        return st

    def block(i, st):
        st = chunks_of_block(i, st, nk)
        finalize(i * nk + nk - 1, st[2], st[3])
        return st

    st = lax.fori_loop(0, nq - 1, block, st)
    st = chunks_of_block(nq - 1, st, nk - 1)
    pv(nc - 1, (nk - 1) % 2)
    finalize(nc - 1, st[0], st[1])


def _mla_flash(qt, k, vt, *, B, S, tq=512, tk=2048):
    T = B * S
    nq, nk = S // tq, S // tk
    assert nk % 2 == 0 and tk // FLASH_QK_ROWS > FLASH_PV_AFTER
    return pl.pallas_call(
        functools.partial(_mla_flash_kernel, tq=tq, tk=tk, nk=nk, nq=nq),
        grid=(B, MLA_HEADS),
        in_specs=[
            pl.BlockSpec((QK_PAD, S), lambda b, h: (h, b)),
            pl.BlockSpec((S, QK_PAD), lambda b, h: (b, h)),
            pl.BlockSpec((VT_ROWS, S), lambda b, h: (h, b)),
        ],
        out_specs=pl.BlockSpec((S, V_HEAD), lambda b, h: (b, h)),
        out_shape=jax.ShapeDtypeStruct((T, MLA_HEADS * V_HEAD), BF16),
        scratch_shapes=[pltpu.VMEM((tk, tq), BF16), pltpu.VMEM((tk, tq), BF16), pltpu.VMEM((VT_ROWS, tq), F32)],
        compiler_params=_cparams(2),
        name="mla_flash",
    )(qt, k, vt)


def _dil_kernel(q_ref, k_ref, v_ref, bias_ref, o_ref, lse_ref, *, L, ta, cr):
    scale = DIL_HEAD_DIM ** -0.5 * LOG2E
    ones = jnp.ones((DIL_KB, DIL_HEAD_DIM), BF16)
    lane = lax.broadcasted_iota(jnp.int32, (DIL_QB, 128), 1)
    base = pl.program_id(2) * ta

    def rows_of(ref, start, n, cs):
        lo = start // cr
        idx = slice(lo, lo + n // cr) if isinstance(start, int) else pl.ds(lo, n // cr)
        return ref[0, idx, 0, :, cs].reshape(n, DIL_HEAD_DIM)

    for t in range(ta // DIL_QB):
        a0 = base + t * DIL_QB
        k0 = jnp.clip(a0 - DIL_SIDE, 0, L - DIL_KB)
        var = (a0 - k0) // DIL_SIDE
        chunks = slice(t * DIL_QB // cr, (t + 1) * DIL_QB // cr)
        lses = []
        for h in range(DIL_HEADS_PER_GROUP):
            cs = slice(h * DIL_HEAD_DIM, (h + 1) * DIL_HEAD_DIM)
            qh = rows_of(q_ref, t * DIL_QB, DIL_QB, cs)
            kh = rows_of(k_ref, k0, DIL_KB, cs)
            vh = rows_of(v_ref, k0, DIL_KB, cs)
            s = _dot_nt(qh, kh) * scale + bias_ref[var, h]
            m = jnp.max(s, axis=-1, keepdims=True)
            p = jnp.exp2(s - m).astype(BF16)
            ov = _dot(p, jnp.concatenate([vh, ones], axis=-1))
            l = ov[:, DIL_HEAD_DIM:]
            o = (ov[:, :DIL_HEAD_DIM] * (1.0 / l)).astype(BF16)
            o_ref[0, chunks, 0, :, cs] = o.reshape(DIL_QB // cr, cr, DIL_HEAD_DIM)
            lses.append(m * (1.0 / LOG2E) + jnp.log(l))
        packed = jnp.where(lane < 32, lses[0], jnp.where(lane < 64, lses[1], jnp.where(lane < 96, lses[2], lses[3])))
        lse_ref[0, chunks, 0, :, :] = packed.reshape(DIL_QB // cr, cr, 128)


def _dilated_group(proj, bias, *, B, S, r, col, name):
    cr = min(PERM_BLOCK // r, DIL_SIDE)
    assert r == 1 or r * cr == PERM_BLOCK
    L = S // r
    per_stream = L // cr
    ta = min(L, 512)
    x5 = proj.reshape(B, per_stream, r, cr, PROJ_W)
    return pl.pallas_call(
        functools.partial(_dil_kernel, L=L, ta=ta, cr=cr),
        grid=(B, r, L // ta),
        in_specs=[
            pl.BlockSpec((1, ta // cr, 1, cr, DIL_W), lambda b, c, a: (b, a, c, 0, col)),
            pl.BlockSpec((1, per_stream, 1, cr, DIL_W), lambda b, c, a: (b, 0, c, 0, col + 1)),
            pl.BlockSpec((1, per_stream, 1, cr, DIL_W), lambda b, c, a: (b, 0, c, 0, col + 2)),
            pl.BlockSpec(bias.shape, lambda b, c, a: (0, 0, 0, 0)),
        ],
        out_specs=[
            pl.BlockSpec((1, ta // cr, 1, cr, DIL_W), lambda b, c, a: (b, a, c, 0, 0)),
            pl.BlockSpec((1, ta // cr, 1, cr, 128), lambda b, c, a: (b, a, c, 0, 0)),
        ],
        out_shape=[
            jax.ShapeDtypeStruct((B, per_stream, r, cr, DIL_W), BF16),
            jax.ShapeDtypeStruct((B, per_stream, r, cr, 128), F32),
        ],
        compiler_params=_cparams(3),
        name=name,
    )(x5, x5, x5, bias)


def _dilated(proj, bias_tabs, *, B, S):
    T = B * S
    ods, lss = [], []
    for g, (_, r) in enumerate(DIL_PAIRS):
        od, ls = _dilated_group(proj, bias_tabs[g], B=B, S=S, r=r, col=(_PROJ_DIL + g * 3 * DIL_W) // DIL_W,
                                name=f"dilated_g{g}")
        ods.append(od.reshape(T, DIL_W))
        lss.append(ls.reshape(T, 128))
    return ods, lss


def _t5_bucket(rel):
    nb = NUM_BUCKETS // 2
    ret = (rel > 0).astype(np.int32) * nb
    n = np.abs(rel)
    max_exact = nb // 2
    large = max_exact + (np.log(np.maximum(n, 1) / max_exact) / np.log(MAX_DISTANCE / max_exact)
                         * (nb - max_exact)).astype(np.int32)
    large = np.minimum(large, nb - 1)
    return (ret + np.where(n < max_exact, n, large)).astype(np.int32)


def _dil_bias_tables(rel_bias):
    qq = np.arange(DIL_QB)[:, None]
    kk = np.arange(DIL_KB)[None, :]
    ext = jnp.concatenate([rel_bias.astype(F32), jnp.full((1, rel_bias.shape[1]), NEG_INF, F32)], axis=0)
    tables = []
    for g, (_, r) in enumerate(DIL_PAIRS):
        idx = []
        for var in range(3):
            j = kk - qq - var * DIL_SIDE
            bucket = _t5_bucket(j * r)
            idx.append(np.where(np.abs(j) <= DIL_SIDE, bucket, NUM_BUCKETS))
        onehot = (np.stack(idx)[..., None] == np.arange(NUM_BUCKETS + 1)).astype(np.float32)
        heads = ext[:, g * DIL_HEADS_PER_GROUP:(g + 1) * DIL_HEADS_PER_GROUP]
        tables.append(jnp.einsum("vqkn,nh->vhqk", onehot, heads * LOG2E, precision=lax.Precision.HIGHEST))
    return tables


def _mem_kv_kernel(m_ref, g_ref, w_ref, o_ref):
    o_ref[...] = _dot(_rms(m_ref[...], g_ref[...]).astype(BF16), w_ref[...]).astype(BF16)


def _mem_kv(mem2, g, w):
    R, D = mem2.shape
    N = w.shape[1]
    return pl.pallas_call(
        _mem_kv_kernel,
        grid=(R // N_MEM,),
        in_specs=[
            pl.BlockSpec((N_MEM, D), lambda i: (i, 0)),
            pl.BlockSpec((1, D), lambda i: (0, 0)),
            pl.BlockSpec((D, N), lambda i: (0, 0)),
        ],
        out_specs=pl.BlockSpec((N_MEM, N), lambda i: (i, 0)),
        out_shape=jax.ShapeDtypeStruct((R, N), BF16),
        compiler_params=_cparams(1),
        name="mem_kv",
    )(mem2, g, w)


def _mem_attn_kernel(q_ref, kv_ref, o_ref):
    scale = X_HEAD_DIM ** -0.5
    hv = X_HEADS * X_HEAD_DIM
    for h in range(X_HEADS):
        cs = slice(h * X_HEAD_DIM, (h + 1) * X_HEAD_DIM)
        s = _dot_nt(q_ref[:, cs], kv_ref[:, cs]) * scale
        m = jnp.max(s, axis=-1, keepdims=True)
        p = jnp.exp(s - m)
        l = jnp.sum(p, axis=-1, keepdims=True)
        o = _dot(p.astype(BF16), kv_ref[:, hv + h * X_HEAD_DIM:hv + (h + 1) * X_HEAD_DIM]) / l
        o_ref[:, cs] = o.astype(BF16)


def _mem_attn(proj, kvm, *, S, tm=512):
    T = proj.shape[0]
    W = X_HEADS * X_HEAD_DIM
    nsb = S // tm
    return pl.pallas_call(
        _mem_attn_kernel,
        grid=(T // tm,),
        in_specs=[
            pl.BlockSpec((tm, W), lambda i: (i, _PROJ_XQ // W)),
            pl.BlockSpec((N_MEM, 2 * W), lambda i: (i // nsb, 0)),
        ],
        out_specs=pl.BlockSpec((tm, W), lambda i: (i, 0)),
        out_shape=jax.ShapeDtypeStruct((T, W), BF16),
        compiler_params=_cparams(1),
        name="mem_attn",
    )(proj, kvm)


def _token_order(pt_ref, od_ref, ls_ref):
    ods, lss = [], []
    for b0 in range(0, od_ref.shape[0], PERM_BLOCK):
        rows = slice(b0, b0 + PERM_BLOCK)
        ods.append(_dot(pt_ref[...], od_ref[rows, :]))
        ls = ls_ref[rows, :]
        hi = ls.astype(BF16)
        rest = ls - hi.astype(F32)
        mid = rest.astype(BF16)
        lo = (rest - mid.astype(F32)).astype(BF16)
        parts = _dot(pt_ref[...], jnp.concatenate([hi, mid, lo], axis=-1))
        lss.append(parts[:, 0:128] + parts[:, 128:256] + parts[:, 256:384])
    return jnp.concatenate(ods, axis=0), jnp.concatenate(lss, axis=0)


def _merge_kernel(gate_ref, omla_ref, od0_ref, od1_ref, od2_ref, ls0_ref, ls1_ref, ls2_ref, omem_ref,
                  wmla_ref, wdil_ref, wmem_ref, pt1_ref, pt2_ref, o_ref):
    od1, ls1 = _token_order(pt1_ref, od1_ref, ls1_ref)
    od2, ls2 = _token_order(pt2_ref, od2_ref, ls2_ref)
    ls = [ls0_ref[...], ls1, ls2]
    mx = jnp.maximum(jnp.maximum(ls[0], ls[1]), ls[2])
    es = [jnp.exp(x - mx) for x in ls]
    inv = 1.0 / (es[0] + es[1] + es[2])
    od = [od0_ref[...].astype(F32), od1, od2]
    parts = []
    for h in range(DIL_HEADS_PER_GROUP):
        cs = slice(h * DIL_HEAD_DIM, (h + 1) * DIL_HEAD_DIM)
        acc = None
        for gi in range(DIL_GROUPS):
            a = (es[gi] * inv)[:, h * 32:h * 32 + 1]
            term = a * od[gi][:, cs]
            acc = term if acc is None else acc + term
        parts.append(acc.astype(BF16))
    o_dil = jnp.concatenate(parts, axis=-1)

    def gate(b):
        return jax.nn.sigmoid(gate_ref[:, b * D_MODEL:(b + 1) * D_MODEL].astype(F32))

    merged = gate(0) * _dot(omla_ref[...], wmla_ref[...])
    merged += gate(1) * _dot(o_dil, wdil_ref[...])
    merged += gate(2) * _dot(omem_ref[...], wmem_ref[...])
    o_ref[...] = merged.astype(BF16)


def _merge(proj, o_mla, ods, lss, o_mem, wmla, wdil, wmem, *, tm=512):
    T = proj.shape[0]
    assert tm % PERM_BLOCK == 0
    row = lambda i: (i, 0)
    const = lambda i: (0, 0)
    wspec = lambda w: pl.BlockSpec(w.shape, const, pipeline_mode=pl.Buffered(1))
    perms_t = [jnp.asarray(_stream_perm(r).T, BF16) for _, r in DIL_PAIRS[1:]]
    return pl.pallas_call(
        _merge_kernel,
        grid=(T // tm,),
        in_specs=[
            pl.BlockSpec((tm, N_BRANCH * D_MODEL), row),
            pl.BlockSpec((tm, o_mla.shape[1]), row),
            pl.BlockSpec((tm, DIL_W), row), pl.BlockSpec((tm, DIL_W), row), pl.BlockSpec((tm, DIL_W), row),
            pl.BlockSpec((tm, 128), row), pl.BlockSpec((tm, 128), row), pl.BlockSpec((tm, 128), row),
            pl.BlockSpec((tm, o_mem.shape[1]), row),
            wspec(wmla), wspec(wdil), wspec(wmem),
            pl.BlockSpec((PERM_BLOCK, PERM_BLOCK), const), pl.BlockSpec((PERM_BLOCK, PERM_BLOCK), const),
        ],
        out_specs=pl.BlockSpec((tm, D_MODEL), row),
        out_shape=jax.ShapeDtypeStruct((T, D_MODEL), BF16),
        compiler_params=_cparams(1),
        name="merge",
    )(proj, o_mla, *ods, *lss, o_mem, wmla, wdil, wmem, *perms_t)


def _out_mlp_kernel(x_ref, mg_ref, wout_ref, gm_ref, wup_ref, wdn_ref, gf_ref, o_ref, h_ref):
    j = pl.program_id(1)

    @pl.when(j == 0)
    def _():
        x1 = x_ref[...] + _dot(mg_ref[...], wout_ref[...])
        o_ref[...] = x1
        h_ref[...] = _rms(x1, gm_ref[...]).astype(BF16)

    u = jnp.maximum(_dot(h_ref[...], wup_ref[...]), 0.0)
    o_ref[...] += _dot((u * u).astype(BF16), wdn_ref[...])

    @pl.when(j == pl.num_programs(1) - 1)
    def _():
        o_ref[...] = _rms(o_ref[...], gf_ref[...])


def _out_mlp(x2, merged, wout, gm, wup, wdn, gf, *, tm=512, tf=1024):
    T, D = x2.shape
    row = lambda i, j: (i, 0)
    const = lambda i, j: (0, 0)
    return pl.pallas_call(
        _out_mlp_kernel,
        grid=(T // tm, D_FF // tf),
        in_specs=[
            pl.BlockSpec((tm, D), row),
            pl.BlockSpec((tm, D), row),
            pl.BlockSpec((D, D), const, pipeline_mode=pl.Buffered(1)),
            pl.BlockSpec((1, D), const),
            pl.BlockSpec((D, tf), lambda i, j: (0, j)),
            pl.BlockSpec((tf, D), lambda i, j: (j, 0)),
            pl.BlockSpec((1, D), const),
        ],
        out_specs=pl.BlockSpec((tm, D), row),
        out_shape=jax.ShapeDtypeStruct((T, D), F32),
        scratch_shapes=[pltpu.VMEM((tm, D), BF16)],
        compiler_params=_cparams(2),
        name="out_mlp",
    )(x2, merged, wout, gm, wup, wdn, gf)


def _rot_half_cols(w):
    half = QK_ROPE // 2
    return jnp.concatenate([-w[..., half:], w[..., :half]], axis=-1)


def _prep_weights(w_in, w_uq, w_ukv, w_mem_kv, w_b_mla, w_b_dil, w_b_mem, w_out, w_up, w_down):
    D = w_in.shape[0]
    w_in = w_in.astype(BF16)
    o = 0
    cq = w_in[:, o:o + Q_LORA]; o += Q_LORA
    ckv = w_in[:, o:o + KV_LORA]; o += KV_LORA
    kr = w_in[:, o:o + QK_ROPE]; o += QK_ROPE
    dil = w_in[:, o:o + 3 * DIL_GROUPS * DIL_W]; o += 3 * DIL_GROUPS * DIL_W
    xq = w_in[:, o:o + X_HEADS * X_HEAD_DIM]; o += X_HEADS * X_HEAD_DIM
    gate = w_in[:, o:o + N_BRANCH * D_MODEL]
    dil = dil.reshape(D, 3, DIL_GROUPS, DIL_W).transpose(0, 2, 1, 3).reshape(D, 3 * DIL_GROUPS * DIL_W)
    z64 = jnp.zeros((D, 64), BF16)
    w_in_p = jnp.concatenate(
        [gate, xq, dil, cq, ckv, kr, z64, _rot_half_cols(kr), z64, jnp.zeros((D, 256), BF16)], axis=1)

    wkv = w_ukv.reshape(KV_LORA, MLA_HEADS, QK_NOPE + V_HEAD)
    wk = wkv[..., :QK_NOPE].reshape(KV_LORA, MLA_HEADS * QK_NOPE).astype(BF16)
    wvt = wkv[..., QK_NOPE:].reshape(KV_LORA, MLA_HEADS * V_HEAD).T.astype(BF16)
    return dict(w_in=w_in_p, wqt=w_uq.T.astype(BF16), wk=wk, wvt=wvt, wmkv=w_mem_kv.astype(BF16),
                wmla=w_b_mla.astype(BF16), wdil=w_b_dil.astype(BF16), wmem=w_b_mem.astype(BF16),
                wout=w_out.astype(BF16), wup=w_up.astype(BF16), wdn=w_down.astype(BF16))


def _rope_tables(S):
    half = QK_ROPE // 2
    inv = 1.0 / (ROPE_THETA ** (jnp.arange(half, dtype=F32) / half))
    ang = jnp.arange(S).astype(F32)[:, None] * inv[None, :]
    z = jnp.zeros((S, 64), F32)
    cos, sin = jnp.cos(ang), jnp.sin(ang)
    return jnp.concatenate([cos, cos, z], axis=1), jnp.concatenate([sin, sin, z], axis=1), cos.T, sin.T


def _trunk(x, mem, W, bias_tabs, g_attn, g_qn, g_kvn, g_mem, g_mlp, g_final):
    B, S, D = x.shape
    T = B * S
    x2 = x.reshape(T, D)
    row = lambda g: g.reshape(1, -1).astype(F32)

    proj = _in_proj(x2, row(g_attn), W["w_in"])

    q_scale = (QK_NOPE + QK_ROPE) ** -0.5 * math.log2(math.e)
    qt, k, vt = _mla_proj(proj, _rope_tables(S), row(g_qn), row(g_kvn), W["wqt"], W["wk"], W["wvt"], S=S,
                          q_scale=q_scale)
    o_mla = _mla_flash(qt, k, vt, B=B, S=S)

    ods, lss = _dilated(proj, bias_tabs, B=B, S=S)

    kvm = _mem_kv(mem.reshape(B * N_MEM, D), row(g_mem), W["wmkv"])
    o_mem = _mem_attn(proj, kvm, S=S)

    merged = _merge(proj, o_mla, ods, lss, o_mem, W["wmla"], W["wdil"], W["wmem"])
    y = _out_mlp(x2, merged, W["wout"], row(g_mlp), W["wup"], W["wdn"], row(g_final))
    return y.reshape(B, S, D)


def kernel(x_prompt, x_sample, mem_prompt, mem_sample, rel_bias, g_attn, w_in, g_q_norm, w_uq, g_kv_norm, w_ukv,
           g_mem, w_mem_kv, w_b_mla, w_b_dil, w_b_mem, w_out, g_mlp, w_up, w_down, g_final):
    assert w_in.shape[0] == 1, "single layer"
    W = _prep_weights(w_in[0], w_uq[0], w_ukv[0], w_mem_kv[0], w_b_mla[0], w_b_dil[0], w_b_mem[0], w_out[0],
                      w_up[0], w_down[0])
    bias_tabs = _dil_bias_tables(rel_bias)
    args = (W, bias_tabs, g_attn[0], g_q_norm[0], g_kv_norm[0], g_mem[0], g_mlp[0], g_final)
    return (_trunk(x_prompt, mem_prompt, *args), _trunk(x_sample, mem_sample, *args))
```

```python
import functools
import math

import numpy as np
import jax
import jax.numpy as jnp
from jax import lax
from jax.experimental import pallas as pl
from jax.experimental.pallas import tpu as pltpu

F32 = jnp.float32
BF16 = jnp.bfloat16

D_MODEL = 2048
N_MEM = 256
MLA_HEADS = 12
Q_LORA = 512
KV_LORA = 512
QK_NOPE = 128
QK_ROPE = 64
V_HEAD = 128
ROPE_THETA = 10000.0
DIL_PAIRS = ((128, 1), (512, 4), (2048, 16))
DIL_GROUPS = 3
DIL_HEADS_PER_GROUP = 4
DIL_HEAD_DIM = 128
X_HEADS = 4
X_HEAD_DIM = 256
NUM_BUCKETS = 32
MAX_DISTANCE = 1024
D_FF = 4 * D_MODEL
N_BRANCH = 3
EPS = 1e-6
NEG_INF = -1e30
LOG2E = math.log2(math.e)

DIL_W = DIL_HEADS_PER_GROUP * DIL_HEAD_DIM
DIL_SIDE = 64
DIL_QB = 128
DIL_KB = DIL_QB + 2 * DIL_SIDE
PERM_BLOCK = 256
QK_PAD = 256
VT_ROWS = V_HEAD + 16

_PROJ_GATE = 0
_PROJ_XQ = _PROJ_GATE + N_BRANCH * D_MODEL
_PROJ_DIL = _PROJ_XQ + X_HEADS * X_HEAD_DIM
_PROJ_CQ = _PROJ_DIL + 3 * DIL_GROUPS * DIL_W
_PROJ_CKV = _PROJ_CQ + Q_LORA
_PROJ_KR = _PROJ_CKV + KV_LORA
PROJ_W = _PROJ_KR + 512

VMEM_LIMIT = 52 * 1024 * 1024


def _cparams(n_axes):
    return pltpu.CompilerParams(dimension_semantics=("arbitrary",) * n_axes, vmem_limit_bytes=VMEM_LIMIT)


def _rms(x32, g32):
    return x32 * lax.rsqrt(jnp.mean(x32 * x32, axis=-1, keepdims=True) + EPS) * g32


def _dot(a, b):
    return jnp.dot(a, b, preferred_element_type=F32)


def _dot_nt(a, b):
    return lax.dot_general(a, b, (((1,), (1,)), ((), ())), preferred_element_type=F32)


def _stream_perm(r):
    cr = PERM_BLOCK // r
    p = np.zeros((PERM_BLOCK, PERM_BLOCK), np.float32)
    for c in range(r):
        for a in range(cr):
            p[c * cr + a, r * a + c] = 1.0
    return p


def _in_proj_kernel(x_ref, g_ref, w_ref, p1_ref, p2_ref, o_ref, h_ref, *, tn):
    j = pl.program_id(1)

    @pl.when(j == 0)
    def _():
        for b0 in range(0, x_ref.shape[0], PERM_BLOCK):
            rows = slice(b0, b0 + PERM_BLOCK)
            h = _rms(x_ref[rows, :], g_ref[...]).astype(BF16)
            h_ref[0, rows, :] = h
            h_ref[1, rows, :] = _dot(p1_ref[...], h).astype(BF16)
            h_ref[2, rows, :] = _dot(p2_ref[...], h).astype(BF16)

    half = tn // 2
    for k in range(2):
        lane0 = j * tn + k * half
        ver = (jnp.where(lane0 >= _PROJ_DIL + 3 * DIL_W, 1, 0) + jnp.where(lane0 >= _PROJ_DIL + 6 * DIL_W, 1, 0)
               - jnp.where(lane0 >= _PROJ_CQ, 2, 0))
        o_ref[:, k * half:(k + 1) * half] = _dot(h_ref[ver], w_ref[:, k * half:(k + 1) * half]).astype(BF16)


def _in_proj(x2, g, w, *, tm=1024, tn=1024):
    T, D = x2.shape
    assert tm % PERM_BLOCK == 0 and (3 * DIL_W) % (tn // 2) == 0 and _PROJ_DIL % (tn // 2) == 0
    perms = [jnp.asarray(_stream_perm(r), BF16) for _, r in DIL_PAIRS[1:]]
    const = lambda i, j: (0, 0)
    return pl.pallas_call(
        functools.partial(_in_proj_kernel, tn=tn),
        grid=(T // tm, PROJ_W // tn),
        in_specs=[
            pl.BlockSpec((tm, D), lambda i, j: (i, 0)),
            pl.BlockSpec((1, D), const),
            pl.BlockSpec((D, tn), lambda i, j: (0, j)),
            pl.BlockSpec((PERM_BLOCK, PERM_BLOCK), const),
            pl.BlockSpec((PERM_BLOCK, PERM_BLOCK), const),
        ],
        out_specs=pl.BlockSpec((tm, tn), lambda i, j: (i, j)),
        out_shape=jax.ShapeDtypeStruct((T, PROJ_W), BF16),
        scratch_shapes=[pltpu.VMEM((3, tm, D), BF16)],
        compiler_params=_cparams(2),
        name="in_proj",
    )(x2, g, w, *perms)


def _mla_proj_kernel(cq_ref, ckv_ref, kr_ref, cos_ref, sin_ref, cost_ref, sint_ref, gq_ref, gkv_ref,
                     wqt_ref, wk_ref, wvt_ref, qt_ref, k_ref, vt_ref, *, q_scale):
    half = QK_ROPE // 2
    hd = QK_NOPE + QK_ROPE
    nq_t = _rms(cq_ref[...].astype(F32), gq_ref[...]).T.astype(BF16)
    q_t = _dot(wqt_ref[...], nq_t)
    cos_t = cost_ref[...]
    sin_t = sint_ref[...]
    zeros = jnp.zeros((QK_PAD - hd, q_t.shape[1]), BF16)
    for h in range(MLA_HEADS):
        r0, o0 = h * hd, h * QK_PAD
        x1 = q_t[r0 + QK_NOPE:r0 + QK_NOPE + half]
        x2 = q_t[r0 + QK_NOPE + half:r0 + hd]
        qt_ref[o0:o0 + QK_NOPE, :] = (q_t[r0:r0 + QK_NOPE] * q_scale).astype(BF16)
        qt_ref[o0 + QK_NOPE:o0 + QK_NOPE + half, :] = ((x1 * cos_t - x2 * sin_t) * q_scale).astype(BF16)
        qt_ref[o0 + QK_NOPE + half:o0 + hd, :] = ((x2 * cos_t + x1 * sin_t) * q_scale).astype(BF16)
        qt_ref[o0 + hd:o0 + QK_PAD, :] = zeros

    nkv = _rms(ckv_ref[...].astype(F32), gkv_ref[...])
    v_t = _dot(wvt_ref[...], nkv.T.astype(BF16)).astype(BF16)
    extra = jnp.where(lax.broadcasted_iota(jnp.int32, (VT_ROWS - V_HEAD, v_t.shape[1]), 0) == 0, 1.0, 0.0).astype(BF16)
    for h in range(MLA_HEADS):
        vt_ref[h * VT_ROWS:h * VT_ROWS + V_HEAD, :] = v_t[h * V_HEAD:(h + 1) * V_HEAD]
        vt_ref[h * VT_ROWS + V_HEAD:(h + 1) * VT_ROWS, :] = extra
    kn = _dot(nkv.astype(BF16), wk_ref[...])
    kr = kr_ref[...].astype(F32)
    kpe = (kr[:, 0:128] * cos_ref[...] + kr[:, 128:256] * sin_ref[...]).astype(BF16)
    for h in range(MLA_HEADS):
        o0 = h * QK_PAD
        k_ref[:, o0:o0 + QK_NOPE] = kn[:, h * QK_NOPE:(h + 1) * QK_NOPE].astype(BF16)
        k_ref[:, o0 + QK_NOPE:o0 + QK_PAD] = kpe


def _mla_proj(proj, rope, gq, gkv, wqt, wk, wvt, *, S, q_scale, tm=512):
    T = proj.shape[0]
    nsb = S // tm
    HQ = MLA_HEADS * QK_PAD
    HV = MLA_HEADS * VT_ROWS
    cos_k, sin_k, cos_t, sin_t = rope
    const = lambda i: (0, 0)
    wspec = lambda w: pl.BlockSpec(w.shape, const, pipeline_mode=pl.Buffered(1))
    return pl.pallas_call(
        functools.partial(_mla_proj_kernel, q_scale=q_scale),
        grid=(T // tm,),
        in_specs=[
            pl.BlockSpec((tm, Q_LORA), lambda i: (i, _PROJ_CQ // 512)),
            pl.BlockSpec((tm, KV_LORA), lambda i: (i, _PROJ_CKV // 512)),
            pl.BlockSpec((tm, 512), lambda i: (i, _PROJ_KR // 512)),
            pl.BlockSpec((tm, 128), lambda i: (i % nsb, 0)),
            pl.BlockSpec((tm, 128), lambda i: (i % nsb, 0)),
            pl.BlockSpec((QK_ROPE // 2, tm), lambda i: (0, i % nsb)),
            pl.BlockSpec((QK_ROPE // 2, tm), lambda i: (0, i % nsb)),
            pl.BlockSpec((1, Q_LORA), const),
            pl.BlockSpec((1, KV_LORA), const),
            wspec(wqt), wspec(wk), wspec(wvt),
        ],
        out_specs=[
            pl.BlockSpec((HQ, tm), lambda i: (0, i)),
            pl.BlockSpec((tm, HQ), lambda i: (i, 0)),
            pl.BlockSpec((HV, tm), lambda i: (0, i)),
        ],
        out_shape=[
            jax.ShapeDtypeStruct((HQ, T), BF16),
            jax.ShapeDtypeStruct((T, HQ), BF16),
            jax.ShapeDtypeStruct((HV, T), BF16),
        ],
        compiler_params=_cparams(1),
        name="mla_proj",
    )(proj, proj, proj, cos_k, sin_k, cos_t, sin_t, gq, gkv, wqt, wk, wvt)


FLASH_SLAB = 32
FLASH_QK_ROWS = 256


FLASH_WINDOW = 100.0
FLASH_PV_AFTER = 4
FLASH_EXACT_ROWS = 512


def _zero_like_bits(v):
    u = pltpu.bitcast(v, jnp.uint32)
    return pltpu.bitcast(lax.shift_right_logical(lax.shift_right_logical(u, jnp.uint32(16)), jnp.uint32(16)), F32)


def _mla_flash_kernel(qt_ref, k_ref, vt_ref, o_ref, p0_ref, p1_ref, acc_ref, accd_ref, guess_ref, *, tq, tk, nk, nq):
    p_refs = (p0_ref, p1_ref)
    nc = nq * nk

    def chunk_pos(c):
        qi = c // nk
        return qi, c == qi * nk, pl.multiple_of((c - qi * nk) * tk, tk), pl.multiple_of(qi * tq, tq)

    def pv(c, slot, first=False):
        r0 = chunk_pos(c)[2]
        d = _dot(vt_ref[:, pl.ds(r0, tk)], p_refs[slot][...])
        if first:
            accd_ref[...] = acc_ref[...]
            acc = d
        else:
            acc = acc_ref[...] + d
        acc_ref[...] = acc
        return acc[0:1]

    def qk_exp(c, slot, st, pv_args=None):
        ref, rmax, done_ref, done_rmax = st
        _, first, r0, c0 = chunk_pos(c)
        done_ref = jnp.where(first, ref, done_ref)
        done_rmax = jnp.where(first, rmax, done_rmax)
        ref = jnp.where(first, rmax, ref)
        rmax = jnp.where(first, -jnp.inf, rmax)
        mx = None
        shift = ref
        for n, q0 in enumerate(range(0, tk, FLASH_QK_ROWS)):
            if n == FLASH_PV_AFTER and pv_args is not None:
                shift = ref + _zero_like_bits(pv(*pv_args))
            s = _dot(k_ref[pl.ds(r0 + q0, FLASH_QK_ROWS), :], qt_ref[:, pl.ds(c0, tq)])
            p_refs[slot][q0:q0 + FLASH_QK_ROWS, :] = jnp.exp2(s - shift).astype(BF16)
            part = jnp.max(s.reshape(FLASH_QK_ROWS // FLASH_SLAB, FLASH_SLAB, tq), axis=0)
            mx = part if mx is None else jnp.maximum(mx, part)
        rmax = jnp.maximum(rmax, jnp.max(mx, axis=0, keepdims=True))
        return ref, rmax, done_ref, done_rmax

    def exact_block(qi):
        qt = qt_ref[:, pl.ds(pl.multiple_of(qi * tq, tq), tq)]

        def step(j, carry):
            m, acc = carry
            r0 = pl.multiple_of(j * FLASH_EXACT_ROWS, FLASH_EXACT_ROWS)
            s = _dot(k_ref[pl.ds(r0, FLASH_EXACT_ROWS), :], qt)
            m_new = jnp.maximum(m, jnp.max(s, axis=0, keepdims=True))
            p = jnp.exp2(s - m_new).astype(BF16)
            return m_new, acc * jnp.exp2(m - m_new) + _dot(vt_ref[:, pl.ds(r0, FLASH_EXACT_ROWS)], p)

        init = (jnp.full((1, tq), -jnp.inf, F32), jnp.zeros((VT_ROWS, tq), F32))
        _, acc = lax.fori_loop(0, nk * tk // FLASH_EXACT_ROWS, step, init)
        accd_ref[...] = acc

    def write_block(qi, a_ref):
        acc = a_ref[...]
        o = acc[:V_HEAD] * (1.0 / acc[V_HEAD:V_HEAD + 1])
        o_ref[pl.ds(pl.multiple_of(qi * tq, tq), tq), :] = o.T.astype(BF16)

    def fallback_if_needed(qi, ref_b, rmax_b):
        in_window = jnp.max(jnp.abs(rmax_b - ref_b)) <= FLASH_WINDOW

        @pl.when(jnp.logical_not(in_window))
        def _():
            exact_block(qi)
            write_block(qi, accd_ref)

    @pl.when(jnp.logical_and(pl.program_id(0) == 0, pl.program_id(1) == 0))
    def _():
        guess_ref[...] = jnp.zeros_like(guess_ref)

    acc_ref[...] = jnp.zeros_like(acc_ref)
    guess = guess_ref[0:1, :]
    st = qk_exp(0, 0, (guess, guess, guess, guess))

    def chunks_of_block(i, st, n):
        for j in range(n):
            c = i * nk + j
            st = qk_exp(c + 1, (j + 1) % 2, st, (c, j % 2, j == 0))
        return st

    def block(i, st):
        ref_p, rmax_p = st[2], st[3]
        st = chunks_of_block(i, st, nk)
        write_block(i - 1, accd_ref)
        fallback_if_needed(i - 1, ref_p, rmax_p)
        return st

    st = chunks_of_block(0, st, nk)
    st = lax.fori_loop(1, nq - 1, block, st)
    ref_p, rmax_p = st[2], st[3]
    st = chunks_of_block(nq - 1, st, nk - 1)
    pv(nc - 1, (nk - 1) % 2)
    write_block(nq - 2, accd_ref)
    fallback_if_needed(nq - 2, ref_p, rmax_p)
    write_block(nq - 1, acc_ref)
    fallback_if_needed(nq - 1, st[0], st[1])
    guess_ref[0:1, :] = st[1]


def _mla_flash(qt, k, vt, *, B, S, tq=512, tk=2048):
    T = B * S
    nq, nk = S // tq, S // tk
    assert nk % 2 == 0 and nq >= 3 and tk // FLASH_QK_ROWS > FLASH_PV_AFTER
    return pl.pallas_call(
        functools.partial(_mla_flash_kernel, tq=tq, tk=tk, nk=nk, nq=nq),
        grid=(B, MLA_HEADS),
        in_specs=[
            pl.BlockSpec((QK_PAD, S), lambda b, h: (h, b)),
            pl.BlockSpec((S, QK_PAD), lambda b, h: (b, h)),
            pl.BlockSpec((VT_ROWS, S), lambda b, h: (h, b)),
        ],
        out_specs=pl.BlockSpec((S, V_HEAD), lambda b, h: (b, h)),
        out_shape=jax.ShapeDtypeStruct((T, MLA_HEADS * V_HEAD), BF16),
        scratch_shapes=[pltpu.VMEM((tk, tq), BF16), pltpu.VMEM((tk, tq), BF16), pltpu.VMEM((VT_ROWS, tq), F32),
                        pltpu.VMEM((VT_ROWS, tq), F32), pltpu.VMEM((8, tq), F32)],
        compiler_params=_cparams(2),
        name="mla_flash",
    )(qt, k, vt)


def _dil_kernel(q_ref, k_ref, v_ref, bias_ref, o_ref, lse_ref, *, L, ta, cr):
    scale = DIL_HEAD_DIM ** -0.5 * LOG2E
    ones = jnp.ones((DIL_KB, DIL_HEAD_DIM), BF16)
    lane = lax.broadcasted_iota(jnp.int32, (DIL_QB, 128), 1)
    base = pl.program_id(2) * ta

    def rows_of(ref, start, n, cs):
        lo = start // cr
        idx = slice(lo, lo + n // cr) if isinstance(start, int) else pl.ds(lo, n // cr)
        return ref[0, idx, 0, :, cs].reshape(n, DIL_HEAD_DIM)

    for t in range(ta // DIL_QB):
        a0 = base + t * DIL_QB
        k0 = jnp.clip(a0 - DIL_SIDE, 0, L - DIL_KB)
        var = (a0 - k0) // DIL_SIDE
        chunks = slice(t * DIL_QB // cr, (t + 1) * DIL_QB // cr)
        lses = []
        for h in range(DIL_HEADS_PER_GROUP):
            cs = slice(h * DIL_HEAD_DIM, (h + 1) * DIL_HEAD_DIM)
            qh = rows_of(q_ref, t * DIL_QB, DIL_QB, cs)
            kh = rows_of(k_ref, k0, DIL_KB, cs)
            vh = rows_of(v_ref, k0, DIL_KB, cs)
            s = _dot_nt(qh, kh) * scale + bias_ref[var, h]
            m = jnp.max(s, axis=-1, keepdims=True)
            p = jnp.exp2(s - m).astype(BF16)
            ov = _dot(p, jnp.concatenate([vh, ones], axis=-1))
            l = ov[:, DIL_HEAD_DIM:]
            o = (ov[:, :DIL_HEAD_DIM] * (1.0 / l)).astype(BF16)
            o_ref[0, chunks, 0, :, cs] = o.reshape(DIL_QB // cr, cr, DIL_HEAD_DIM)
            lses.append(m * (1.0 / LOG2E) + jnp.log(l))
        packed = jnp.where(lane < 32, lses[0], jnp.where(lane < 64, lses[1], jnp.where(lane < 96, lses[2], lses[3])))
        lse_ref[0, chunks, 0, :, :] = packed.reshape(DIL_QB // cr, cr, 128)


def _dilated_group(proj, bias, *, B, S, r, col, name):
    cr = min(PERM_BLOCK // r, DIL_SIDE)
    assert r == 1 or r * cr == PERM_BLOCK
    L = S // r
    per_stream = L // cr
    ta = min(L, 512)
    x5 = proj.reshape(B, per_stream, r, cr, PROJ_W)
    return pl.pallas_call(
        functools.partial(_dil_kernel, L=L, ta=ta, cr=cr),
        grid=(B, r, L // ta),
        in_specs=[
            pl.BlockSpec((1, ta // cr, 1, cr, DIL_W), lambda b, c, a: (b, a, c, 0, col)),
            pl.BlockSpec((1, per_stream, 1, cr, DIL_W), lambda b, c, a: (b, 0, c, 0, col + 1)),
            pl.BlockSpec((1, per_stream, 1, cr, DIL_W), lambda b, c, a: (b, 0, c, 0, col + 2)),
            pl.BlockSpec(bias.shape, lambda b, c, a: (0, 0, 0, 0)),
        ],
        out_specs=[
            pl.BlockSpec((1, ta // cr, 1, cr, DIL_W), lambda b, c, a: (b, a, c, 0, 0)),
            pl.BlockSpec((1, ta // cr, 1, cr, 128), lambda b, c, a: (b, a, c, 0, 0)),
        ],
        out_shape=[
            jax.ShapeDtypeStruct((B, per_stream, r, cr, DIL_W), BF16),
            jax.ShapeDtypeStruct((B, per_stream, r, cr, 128), F32),
        ],
        compiler_params=_cparams(3),
        name=name,
    )(x5, x5, x5, bias)


def _dilated(proj, bias_tabs, *, B, S):
    T = B * S
    ods, lss = [], []
    for g, (_, r) in enumerate(DIL_PAIRS):
        od, ls = _dilated_group(proj, bias_tabs[g], B=B, S=S, r=r, col=(_PROJ_DIL + g * 3 * DIL_W) // DIL_W,
                                name=f"dilated_g{g}")
        ods.append(od.reshape(T, DIL_W))
        lss.append(ls.reshape(T, 128))
    return ods, lss


def _t5_bucket(rel):
    nb = NUM_BUCKETS // 2
    ret = (rel > 0).astype(np.int32) * nb
    n = np.abs(rel)
    max_exact = nb // 2
    large = max_exact + (np.log(np.maximum(n, 1) / max_exact) / np.log(MAX_DISTANCE / max_exact)
                         * (nb - max_exact)).astype(np.int32)
    large = np.minimum(large, nb - 1)
    return (ret + np.where(n < max_exact, n, large)).astype(np.int32)


def _dil_bias_tables(rel_bias):
    qq = np.arange(DIL_QB)[:, None]
    kk = np.arange(DIL_KB)[None, :]
    ext = jnp.concatenate([rel_bias.astype(F32), jnp.full((1, rel_bias.shape[1]), NEG_INF, F32)], axis=0)
    tables = []
    for g, (_, r) in enumerate(DIL_PAIRS):
        idx = []
        for var in range(3):
            j = kk - qq - var * DIL_SIDE
            bucket = _t5_bucket(j * r)
            idx.append(np.where(np.abs(j) <= DIL_SIDE, bucket, NUM_BUCKETS))
        onehot = (np.stack(idx)[..., None] == np.arange(NUM_BUCKETS + 1)).astype(np.float32)
        heads = ext[:, g * DIL_HEADS_PER_GROUP:(g + 1) * DIL_HEADS_PER_GROUP]
        tables.append(jnp.einsum("vqkn,nh->vhqk", onehot, heads * LOG2E, precision=lax.Precision.HIGHEST))
    return tables


def _mem_kv_kernel(m_ref, g_ref, w_ref, o_ref):
    o_ref[...] = _dot(_rms(m_ref[...], g_ref[...]).astype(BF16), w_ref[...]).astype(BF16)


def _mem_kv(mem2, g, w):
    R, D = mem2.shape
    N = w.shape[1]
    return pl.pallas_call(
        _mem_kv_kernel,
        grid=(R // N_MEM,),
        in_specs=[
            pl.BlockSpec((N_MEM, D), lambda i: (i, 0)),
            pl.BlockSpec((1, D), lambda i: (0, 0)),
            pl.BlockSpec((D, N), lambda i: (0, 0)),
        ],
        out_specs=pl.BlockSpec((N_MEM, N), lambda i: (i, 0)),
        out_shape=jax.ShapeDtypeStruct((R, N), BF16),
        compiler_params=_cparams(1),
        name="mem_kv",
    )(mem2, g, w)


def _mem_attn_kernel(q_ref, kv_ref, o_ref):
    scale = X_HEAD_DIM ** -0.5
    hv = X_HEADS * X_HEAD_DIM
    for h in range(X_HEADS):
        cs = slice(h * X_HEAD_DIM, (h + 1) * X_HEAD_DIM)
        s = _dot_nt(q_ref[:, cs], kv_ref[:, cs]) * scale
        m = jnp.max(s, axis=-1, keepdims=True)
        p = jnp.exp(s - m)
        l = jnp.sum(p, axis=-1, keepdims=True)
        o = _dot(p.astype(BF16), kv_ref[:, hv + h * X_HEAD_DIM:hv + (h + 1) * X_HEAD_DIM]) / l
        o_ref[:, cs] = o.astype(BF16)


def _mem_attn(proj, kvm, *, S, tm=512):
    T = proj.shape[0]
    W = X_HEADS * X_HEAD_DIM
    nsb = S // tm
    return pl.pallas_call(
        _mem_attn_kernel,
        grid=(T // tm,),
        in_specs=[
            pl.BlockSpec((tm, W), lambda i: (i, _PROJ_XQ // W)),
            pl.BlockSpec((N_MEM, 2 * W), lambda i: (i // nsb, 0)),
        ],
        out_specs=pl.BlockSpec((tm, W), lambda i: (i, 0)),
        out_shape=jax.ShapeDtypeStruct((T, W), BF16),
        compiler_params=_cparams(1),
        name="mem_attn",
    )(proj, kvm)


def _token_order(pt_ref, od_ref, ls_ref):
    ods, lss = [], []
    for b0 in range(0, od_ref.shape[0], PERM_BLOCK):
        rows = slice(b0, b0 + PERM_BLOCK)
        ods.append(_dot(pt_ref[...], od_ref[rows, :]))
        ls = ls_ref[rows, :]
        hi = ls.astype(BF16)
        rest = ls - hi.astype(F32)
        mid = rest.astype(BF16)
        lo = (rest - mid.astype(F32)).astype(BF16)
        parts = _dot(pt_ref[...], jnp.concatenate([hi, mid, lo], axis=-1))
        lss.append(parts[:, 0:128] + parts[:, 128:256] + parts[:, 256:384])
    return jnp.concatenate(ods, axis=0), jnp.concatenate(lss, axis=0)


def _merge_kernel(gate_ref, omla_ref, od0_ref, od1_ref, od2_ref, ls0_ref, ls1_ref, ls2_ref, omem_ref,
                  wmla_ref, wdil_ref, wmem_ref, pt1_ref, pt2_ref, o_ref):
    od1, ls1 = _token_order(pt1_ref, od1_ref, ls1_ref)
    od2, ls2 = _token_order(pt2_ref, od2_ref, ls2_ref)
    ls = [ls0_ref[...], ls1, ls2]
    mx = jnp.maximum(jnp.maximum(ls[0], ls[1]), ls[2])
    es = [jnp.exp(x - mx) for x in ls]
    inv = 1.0 / (es[0] + es[1] + es[2])
    od = [od0_ref[...].astype(F32), od1, od2]
    parts = []
    for h in range(DIL_HEADS_PER_GROUP):
        cs = slice(h * DIL_HEAD_DIM, (h + 1) * DIL_HEAD_DIM)
        acc = None
        for gi in range(DIL_GROUPS):
            a = (es[gi] * inv)[:, h * 32:h * 32 + 1]
            term = a * od[gi][:, cs]
            acc = term if acc is None else acc + term
        parts.append(acc.astype(BF16))
    o_dil = jnp.concatenate(parts, axis=-1)

    def gate(b):
        return jax.nn.sigmoid(gate_ref[:, b * D_MODEL:(b + 1) * D_MODEL].astype(F32))

    merged = gate(0) * _dot(omla_ref[...], wmla_ref[...])
    merged += gate(1) * _dot(o_dil, wdil_ref[...])
    merged += gate(2) * _dot(omem_ref[...], wmem_ref[...])
    o_ref[...] = merged.astype(BF16)


def _merge(proj, o_mla, ods, lss, o_mem, wmla, wdil, wmem, *, tm=512):
    T = proj.shape[0]
    assert tm % PERM_BLOCK == 0
    row = lambda i: (i, 0)
    const = lambda i: (0, 0)
    wspec = lambda w: pl.BlockSpec(w.shape, const, pipeline_mode=pl.Buffered(1))
    perms_t = [jnp.asarray(_stream_perm(r).T, BF16) for _, r in DIL_PAIRS[1:]]
    return pl.pallas_call(
        _merge_kernel,
        grid=(T // tm,),
        in_specs=[
            pl.BlockSpec((tm, N_BRANCH * D_MODEL), row),
            pl.BlockSpec((tm, o_mla.shape[1]), row),
            pl.BlockSpec((tm, DIL_W), row), pl.BlockSpec((tm, DIL_W), row), pl.BlockSpec((tm, DIL_W), row),
            pl.BlockSpec((tm, 128), row), pl.BlockSpec((tm, 128), row), pl.BlockSpec((tm, 128), row),
            pl.BlockSpec((tm, o_mem.shape[1]), row),
            wspec(wmla), wspec(wdil), wspec(wmem),
            pl.BlockSpec((PERM_BLOCK, PERM_BLOCK), const), pl.BlockSpec((PERM_BLOCK, PERM_BLOCK), const),
        ],
        out_specs=pl.BlockSpec((tm, D_MODEL), row),
        out_shape=jax.ShapeDtypeStruct((T, D_MODEL), BF16),
        compiler_params=_cparams(1),
        name="merge",
    )(proj, o_mla, *ods, *lss, o_mem, wmla, wdil, wmem, *perms_t)


def _out_mlp_kernel(x_ref, mg_ref, wout_ref, gm_ref, wup_ref, wdn_ref, gf_ref, o_ref, h_ref):
    j = pl.program_id(1)

    @pl.when(j == 0)
    def _():
        x1 = x_ref[...] + _dot(mg_ref[...], wout_ref[...])
        o_ref[...] = x1
        h_ref[...] = _rms(x1, gm_ref[...]).astype(BF16)

    u = jnp.maximum(_dot(h_ref[...], wup_ref[...]), 0.0)
    o_ref[...] += _dot((u * u).astype(BF16), wdn_ref[...])

    @pl.when(j == pl.num_programs(1) - 1)
    def _():
        o_ref[...] = _rms(o_ref[...], gf_ref[...])


def _out_mlp(x2, merged, wout, gm, wup, wdn, gf, *, tm=512, tf=1024):
    T, D = x2.shape
    row = lambda i, j: (i, 0)
    const = lambda i, j: (0, 0)
    return pl.pallas_call(
        _out_mlp_kernel,
        grid=(T // tm, D_FF // tf),
        in_specs=[
            pl.BlockSpec((tm, D), row),
            pl.BlockSpec((tm, D), row),
            pl.BlockSpec((D, D), const, pipeline_mode=pl.Buffered(1)),
            pl.BlockSpec((1, D), const),
            pl.BlockSpec((D, tf), lambda i, j: (0, j)),
            pl.BlockSpec((tf, D), lambda i, j: (j, 0)),
            pl.BlockSpec((1, D), const),
        ],
        out_specs=pl.BlockSpec((tm, D), row),
        out_shape=jax.ShapeDtypeStruct((T, D), F32),
        scratch_shapes=[pltpu.VMEM((tm, D), BF16)],
        compiler_params=_cparams(2),
        name="out_mlp",
    )(x2, merged, wout, gm, wup, wdn, gf)


def _rot_half_cols(w):
    half = QK_ROPE // 2
    return jnp.concatenate([-w[..., half:], w[..., :half]], axis=-1)


def _prep_weights(w_in, w_uq, w_ukv, w_mem_kv, w_b_mla, w_b_dil, w_b_mem, w_out, w_up, w_down):
    D = w_in.shape[0]
    w_in = w_in.astype(BF16)
    o = 0
    cq = w_in[:, o:o + Q_LORA]; o += Q_LORA
    ckv = w_in[:, o:o + KV_LORA]; o += KV_LORA
    kr = w_in[:, o:o + QK_ROPE]; o += QK_ROPE
    dil = w_in[:, o:o + 3 * DIL_GROUPS * DIL_W]; o += 3 * DIL_GROUPS * DIL_W
    xq = w_in[:, o:o + X_HEADS * X_HEAD_DIM]; o += X_HEADS * X_HEAD_DIM
    gate = w_in[:, o:o + N_BRANCH * D_MODEL]
    dil = dil.reshape(D, 3, DIL_GROUPS, DIL_W).transpose(0, 2, 1, 3).reshape(D, 3 * DIL_GROUPS * DIL_W)
    z64 = jnp.zeros((D, 64), BF16)
    w_in_p = jnp.concatenate(
        [gate, xq, dil, cq, ckv, kr, z64, _rot_half_cols(kr), z64, jnp.zeros((D, 256), BF16)], axis=1)

    wkv = w_ukv.reshape(KV_LORA, MLA_HEADS, QK_NOPE + V_HEAD)
    wk = wkv[..., :QK_NOPE].reshape(KV_LORA, MLA_HEADS * QK_NOPE).astype(BF16)
    wvt = wkv[..., QK_NOPE:].reshape(KV_LORA, MLA_HEADS * V_HEAD).T.astype(BF16)
    return dict(w_in=w_in_p, wqt=w_uq.T.astype(BF16), wk=wk, wvt=wvt, wmkv=w_mem_kv.astype(BF16),
                wmla=w_b_mla.astype(BF16), wdil=w_b_dil.astype(BF16), wmem=w_b_mem.astype(BF16),
                wout=w_out.astype(BF16), wup=w_up.astype(BF16), wdn=w_down.astype(BF16))


def _rope_tables(S):
    half = QK_ROPE // 2
    inv = 1.0 / (ROPE_THETA ** (jnp.arange(half, dtype=F32) / half))
    ang = jnp.arange(S).astype(F32)[:, None] * inv[None, :]
    z = jnp.zeros((S, 64), F32)
    cos, sin = jnp.cos(ang), jnp.sin(ang)
    return jnp.concatenate([cos, cos, z], axis=1), jnp.concatenate([sin, sin, z], axis=1), cos.T, sin.T


def _trunk(x, mem, W, bias_tabs, g_attn, g_qn, g_kvn, g_mem, g_mlp, g_final):
    B, S, D = x.shape
    T = B * S
    x2 = x.reshape(T, D)
    row = lambda g: g.reshape(1, -1).astype(F32)

    proj = _in_proj(x2, row(g_attn), W["w_in"])

    q_scale = (QK_NOPE + QK_ROPE) ** -0.5 * math.log2(math.e)
    qt, k, vt = _mla_proj(proj, _rope_tables(S), row(g_qn), row(g_kvn), W["wqt"], W["wk"], W["wvt"], S=S,
                          q_scale=q_scale)
    o_mla = _mla_flash(qt, k, vt, B=B, S=S)

    ods, lss = _dilated(proj, bias_tabs, B=B, S=S)

    kvm = _mem_kv(mem.reshape(B * N_MEM, D), row(g_mem), W["wmkv"])
    o_mem = _mem_attn(proj, kvm, S=S)

    merged = _merge(proj, o_mla, ods, lss, o_mem, W["wmla"], W["wdil"], W["wmem"])
    y = _out_mlp(x2, merged, W["wout"], row(g_mlp), W["wup"], W["wdn"], row(g_final))
    return y.reshape(B, S, D)


def kernel(x_prompt, x_sample, mem_prompt, mem_sample, rel_bias, g_attn, w_in, g_q_norm, w_uq, g_kv_norm, w_ukv,
           g_mem, w_mem_kv, w_b_mla, w_b_dil, w_b_mem, w_out, g_mlp, w_up, w_down, g_final):
    assert w_in.shape[0] == 1, "single layer"
    W = _prep_weights(w_in[0], w_uq[0], w_ukv[0], w_mem_kv[0], w_b_mla[0], w_b_dil[0], w_b_mem[0], w_out[0],
                      w_up[0], w_down[0])
    bias_tabs = _dil_bias_tables(rel_bias)
    args = (W, bias_tabs, g_attn[0], g_q_norm[0], g_kv_norm[0], g_mem[0], g_mlp[0], g_final)
    return (_trunk(x_prompt, mem_prompt, *args), _trunk(x_sample, mem_sample, *args))
```

```python
import functools
import math

import numpy as np
import jax
import jax.numpy as jnp
from jax import lax
from jax.experimental import pallas as pl
from jax.experimental.pallas import tpu as pltpu

F32 = jnp.float32
BF16 = jnp.bfloat16

D_MODEL = 2048
N_MEM = 256
MLA_HEADS = 12
Q_LORA = 512
KV_LORA = 512
QK_NOPE = 128
QK_ROPE = 64
V_HEAD = 128
ROPE_THETA = 10000.0
DIL_PAIRS = ((128, 1), (512, 4), (2048, 16))
DIL_GROUPS = 3
DIL_HEADS_PER_GROUP = 4
DIL_HEAD_DIM = 128
X_HEADS = 4
X_HEAD_DIM = 256
NUM_BUCKETS = 32
MAX_DISTANCE = 1024
D_FF = 4 * D_MODEL
N_BRANCH = 3
EPS = 1e-6
NEG_INF = -1e30
LOG2E = math.log2(math.e)

DIL_W = DIL_HEADS_PER_GROUP * DIL_HEAD_DIM
DIL_SIDE = 64
DIL_QB = 128
DIL_KB = DIL_QB + 2 * DIL_SIDE
PERM_BLOCK = 256
QK_PAD = 256
VT_ROWS = V_HEAD + 16

_PROJ_GATE = 0
_PROJ_XQ = _PROJ_GATE + N_BRANCH * D_MODEL
_PROJ_DIL = _PROJ_XQ + X_HEADS * X_HEAD_DIM
_PROJ_CQ = _PROJ_DIL + 3 * DIL_GROUPS * DIL_W
_PROJ_CKV = _PROJ_CQ + Q_LORA
_PROJ_KR = _PROJ_CKV + KV_LORA
PROJ_W = _PROJ_KR + 512

VMEM_LIMIT = 52 * 1024 * 1024


def _cparams(n_axes):
    return pltpu.CompilerParams(dimension_semantics=("arbitrary",) * n_axes, vmem_limit_bytes=VMEM_LIMIT)


def _rms(x32, g32):
    return x32 * lax.rsqrt(jnp.mean(x32 * x32, axis=-1, keepdims=True) + EPS) * g32


def _dot(a, b):
    return jnp.dot(a, b, preferred_element_type=F32)


def _dot_nt(a, b):
    return lax.dot_general(a, b, (((1,), (1,)), ((), ())), preferred_element_type=F32)


def _stream_perm(r):
    cr = PERM_BLOCK // r
    p = np.zeros((PERM_BLOCK, PERM_BLOCK), np.float32)
    for c in range(r):
        for a in range(cr):
            p[c * cr + a, r * a + c] = 1.0
    return p


def _in_proj_kernel(x_ref, g_ref, w_ref, p1_ref, p2_ref, o_ref, h_ref, *, tn):
    j = pl.program_id(1)

    @pl.when(j == 0)
    def _():
        for b0 in range(0, x_ref.shape[0], PERM_BLOCK):
            rows = slice(b0, b0 + PERM_BLOCK)
            h = _rms(x_ref[rows, :], g_ref[...]).astype(BF16)
            h_ref[0, rows, :] = h
            h_ref[1, rows, :] = _dot(p1_ref[...], h).astype(BF16)
            h_ref[2, rows, :] = _dot(p2_ref[...], h).astype(BF16)

    half = tn // 2
    for k in range(2):
        lane0 = j * tn + k * half
        ver = (jnp.where(lane0 >= _PROJ_DIL + 3 * DIL_W, 1, 0) + jnp.where(lane0 >= _PROJ_DIL + 6 * DIL_W, 1, 0)
               - jnp.where(lane0 >= _PROJ_CQ, 2, 0))
        o_ref[:, k * half:(k + 1) * half] = _dot(h_ref[ver], w_ref[:, k * half:(k + 1) * half]).astype(BF16)


def _in_proj(x2, g, w, *, tm=1024, tn=1024):
    T, D = x2.shape
    assert tm % PERM_BLOCK == 0 and (3 * DIL_W) % (tn // 2) == 0 and _PROJ_DIL % (tn // 2) == 0
    perms = [jnp.asarray(_stream_perm(r), BF16) for _, r in DIL_PAIRS[1:]]
    const = lambda i, j: (0, 0)
    return pl.pallas_call(
        functools.partial(_in_proj_kernel, tn=tn),
        grid=(T // tm, PROJ_W // tn),
        in_specs=[
            pl.BlockSpec((tm, D), lambda i, j: (i, 0)),
            pl.BlockSpec((1, D), const),
            pl.BlockSpec((D, tn), lambda i, j: (0, j)),
            pl.BlockSpec((PERM_BLOCK, PERM_BLOCK), const),
            pl.BlockSpec((PERM_BLOCK, PERM_BLOCK), const),
        ],
        out_specs=pl.BlockSpec((tm, tn), lambda i, j: (i, j)),
        out_shape=jax.ShapeDtypeStruct((T, PROJ_W), BF16),
        scratch_shapes=[pltpu.VMEM((3, tm, D), BF16)],
        compiler_params=_cparams(2),
        name="in_proj",
    )(x2, g, w, *perms)


def _mla_proj_kernel(cq_ref, ckv_ref, kr_ref, cos_ref, sin_ref, cost_ref, sint_ref, gq_ref, gkv_ref,
                     wqt_ref, wk_ref, wvt_ref, qt_ref, k_ref, vt_ref, *, q_scale):
    half = QK_ROPE // 2
    hd = QK_NOPE + QK_ROPE
    nq_t = _rms(cq_ref[...].astype(F32), gq_ref[...]).T.astype(BF16)
    q_t = _dot(wqt_ref[...], nq_t)
    cos_t = cost_ref[...]
    sin_t = sint_ref[...]
    zeros = jnp.zeros((QK_PAD - hd, q_t.shape[1]), BF16)
    for h in range(MLA_HEADS):
        r0, o0 = h * hd, h * QK_PAD
        x1 = q_t[r0 + QK_NOPE:r0 + QK_NOPE + half]
        x2 = q_t[r0 + QK_NOPE + half:r0 + hd]
        qt_ref[o0:o0 + QK_NOPE, :] = (q_t[r0:r0 + QK_NOPE] * q_scale).astype(BF16)
        qt_ref[o0 + QK_NOPE:o0 + QK_NOPE + half, :] = ((x1 * cos_t - x2 * sin_t) * q_scale).astype(BF16)
        qt_ref[o0 + QK_NOPE + half:o0 + hd, :] = ((x2 * cos_t + x1 * sin_t) * q_scale).astype(BF16)
        qt_ref[o0 + hd:o0 + QK_PAD, :] = zeros

    nkv = _rms(ckv_ref[...].astype(F32), gkv_ref[...])
    v_t = _dot(wvt_ref[...], nkv.T.astype(BF16)).astype(BF16)
    extra = jnp.where(lax.broadcasted_iota(jnp.int32, (VT_ROWS - V_HEAD, v_t.shape[1]), 0) == 0, 1.0, 0.0).astype(BF16)
    for h in range(MLA_HEADS):
        vt_ref[h * VT_ROWS:h * VT_ROWS + V_HEAD, :] = v_t[h * V_HEAD:(h + 1) * V_HEAD]
        vt_ref[h * VT_ROWS + V_HEAD:(h + 1) * VT_ROWS, :] = extra
    kn = _dot(nkv.astype(BF16), wk_ref[...])
    kr = kr_ref[...].astype(F32)
    kpe = (kr[:, 0:128] * cos_ref[...] + kr[:, 128:256] * sin_ref[...]).astype(BF16)
    for h in range(MLA_HEADS):
        o0 = h * QK_PAD
        k_ref[:, o0:o0 + QK_NOPE] = kn[:, h * QK_NOPE:(h + 1) * QK_NOPE].astype(BF16)
        k_ref[:, o0 + QK_NOPE:o0 + QK_PAD] = kpe


def _mla_proj(proj, rope, gq, gkv, wqt, wk, wvt, *, S, q_scale, tm=512):
    T = proj.shape[0]
    nsb = S // tm
    HQ = MLA_HEADS * QK_PAD
    HV = MLA_HEADS * VT_ROWS
    cos_k, sin_k, cos_t, sin_t = rope
    const = lambda i: (0, 0)
    wspec = lambda w: pl.BlockSpec(w.shape, const, pipeline_mode=pl.Buffered(1))
    return pl.pallas_call(
        functools.partial(_mla_proj_kernel, q_scale=q_scale),
        grid=(T // tm,),
        in_specs=[
            pl.BlockSpec((tm, Q_LORA), lambda i: (i, _PROJ_CQ // 512)),
            pl.BlockSpec((tm, KV_LORA), lambda i: (i, _PROJ_CKV // 512)),
            pl.BlockSpec((tm, 512), lambda i: (i, _PROJ_KR // 512)),
            pl.BlockSpec((tm, 128), lambda i: (i % nsb, 0)),
            pl.BlockSpec((tm, 128), lambda i: (i % nsb, 0)),
            pl.BlockSpec((QK_ROPE // 2, tm), lambda i: (0, i % nsb)),
            pl.BlockSpec((QK_ROPE // 2, tm), lambda i: (0, i % nsb)),
            pl.BlockSpec((1, Q_LORA), const),
            pl.BlockSpec((1, KV_LORA), const),
            wspec(wqt), wspec(wk), wspec(wvt),
        ],
        out_specs=[
            pl.BlockSpec((HQ, tm), lambda i: (0, i)),
            pl.BlockSpec((tm, HQ), lambda i: (i, 0)),
            pl.BlockSpec((HV, tm), lambda i: (0, i)),
        ],
        out_shape=[
            jax.ShapeDtypeStruct((HQ, T), BF16),
            jax.ShapeDtypeStruct((T, HQ), BF16),
            jax.ShapeDtypeStruct((HV, T), BF16),
        ],
        compiler_params=_cparams(1),
        name="mla_proj",
    )(proj, proj, proj, cos_k, sin_k, cos_t, sin_t, gq, gkv, wqt, wk, wvt)


FLASH_SLAB = 32
FLASH_QK_ROWS = 256


FLASH_WINDOW = 100.0
FLASH_PV_AFTER = 4
FLASH_EXACT_ROWS = 512


def _zero_like_bits(v):
    u = pltpu.bitcast(v, jnp.uint32)
    return pltpu.bitcast(lax.shift_right_logical(lax.shift_right_logical(u, jnp.uint32(16)), jnp.uint32(16)), F32)


def _mla_flash_kernel(qt_ref, k_ref, vt_ref, o_ref, p0_ref, p1_ref, acc_ref, accd_ref, guess_ref, *, tq, tk, nk, nq):
    p_refs = (p0_ref, p1_ref)
    nc = nq * nk

    def chunk_pos(c):
        qi = c // nk
        return qi, c == qi * nk, pl.multiple_of((c - qi * nk) * tk, tk), pl.multiple_of(qi * tq, tq)

    def pv(c, slot, first=False):
        r0 = chunk_pos(c)[2]
        d = _dot(vt_ref[:, pl.ds(r0, tk)], p_refs[slot][...])
        if first:
            accd_ref[...] = acc_ref[...]
            acc = d
        else:
            acc = acc_ref[...] + d
        acc_ref[...] = acc
        return acc[0:1]

    def qk_exp(c, slot, st, pv_args=None):
        ref, rmax, done_ref, done_rmax = st
        _, first, r0, c0 = chunk_pos(c)
        done_ref = jnp.where(first, ref, done_ref)
        done_rmax = jnp.where(first, rmax, done_rmax)
        ref = jnp.where(first, rmax, ref)
        rmax = jnp.where(first, -jnp.inf, rmax)
        mx = None
        shift = ref
        for n, q0 in enumerate(range(0, tk, FLASH_QK_ROWS)):
            if n == FLASH_PV_AFTER and pv_args is not None:
                shift = ref + _zero_like_bits(pv(*pv_args))
            s = _dot(k_ref[pl.ds(r0 + q0, FLASH_QK_ROWS), :], qt_ref[:, pl.ds(c0, tq)])
            p_refs[slot][q0:q0 + FLASH_QK_ROWS, :] = jnp.exp2(s - shift).astype(BF16)
            part = jnp.max(s.reshape(FLASH_QK_ROWS // FLASH_SLAB, FLASH_SLAB, tq), axis=0)
            mx = part if mx is None else jnp.maximum(mx, part)
        rmax = jnp.maximum(rmax, jnp.max(mx, axis=0, keepdims=True))
        return ref, rmax, done_ref, done_rmax

    def exact_block(qi):
        qt = qt_ref[:, pl.ds(pl.multiple_of(qi * tq, tq), tq)]

        def step(j, carry):
            m, acc = carry
            r0 = pl.multiple_of(j * FLASH_EXACT_ROWS, FLASH_EXACT_ROWS)
            s = _dot(k_ref[pl.ds(r0, FLASH_EXACT_ROWS), :], qt)
            m_new = jnp.maximum(m, jnp.max(s, axis=0, keepdims=True))
            p = jnp.exp2(s - m_new).astype(BF16)
            return m_new, acc * jnp.exp2(m - m_new) + _dot(vt_ref[:, pl.ds(r0, FLASH_EXACT_ROWS)], p)

        init = (jnp.full((1, tq), -jnp.inf, F32), jnp.zeros((VT_ROWS, tq), F32))
        _, acc = lax.fori_loop(0, nk * tk // FLASH_EXACT_ROWS, step, init)
        accd_ref[...] = acc

    def write_block(qi, a_ref):
        acc = a_ref[...]
        o = acc[:V_HEAD] * (1.0 / acc[V_HEAD:V_HEAD + 1])
        o_ref[pl.ds(pl.multiple_of(qi * tq, tq), tq), :] = o.T.astype(BF16)

    def fallback_if_needed(qi, ref_b, rmax_b):
        in_window = jnp.max(jnp.abs(rmax_b - ref_b)) <= FLASH_WINDOW

        @pl.when(jnp.logical_not(in_window))
        def _():
            exact_block(qi)
            write_block(qi, accd_ref)

    @pl.when(jnp.logical_and(pl.program_id(0) == 0, pl.program_id(1) == 0))
    def _():
        guess_ref[...] = jnp.zeros_like(guess_ref)

    acc_ref[...] = jnp.zeros_like(acc_ref)
    guess = guess_ref[0:1, :]
    st = qk_exp(0, 0, (guess, guess, guess, guess))

    def chunks_of_block(i, st, n):
        for j in range(n):
            c = i * nk + j
            st = qk_exp(c + 1, (j + 1) % 2, st, (c, j % 2, j == 0))
        return st

    def block(i, st):
        ref_p, rmax_p = st[2], st[3]
        st = chunks_of_block(i, st, nk)
        write_block(i - 1, accd_ref)
        fallback_if_needed(i - 1, ref_p, rmax_p)
        return st

    st = chunks_of_block(0, st, nk)
    st = lax.fori_loop(1, nq - 1, block, st)
    ref_p, rmax_p = st[2], st[3]
    st = chunks_of_block(nq - 1, st, nk - 1)
    pv(nc - 1, (nk - 1) % 2)
    write_block(nq - 2, accd_ref)
    fallback_if_needed(nq - 2, ref_p, rmax_p)
    write_block(nq - 1, acc_ref)
    fallback_if_needed(nq - 1, st[0], st[1])
    guess_ref[0:1, :] = st[1]


def _mla_flash(qt, k, vt, *, B, S, tq=512, tk=2048):
    T = B * S
    nq, nk = S // tq, S // tk
    assert nk % 2 == 0 and nq >= 3 and tk // FLASH_QK_ROWS > FLASH_PV_AFTER
    return pl.pallas_call(
        functools.partial(_mla_flash_kernel, tq=tq, tk=tk, nk=nk, nq=nq),
        grid=(B, MLA_HEADS),
        in_specs=[
            pl.BlockSpec((QK_PAD, S), lambda b, h: (h, b)),
            pl.BlockSpec((S, QK_PAD), lambda b, h: (b, h)),
            pl.BlockSpec((VT_ROWS, S), lambda b, h: (h, b)),
        ],
        out_specs=pl.BlockSpec((S, V_HEAD), lambda b, h: (b, h)),
        out_shape=jax.ShapeDtypeStruct((T, MLA_HEADS * V_HEAD), BF16),
        scratch_shapes=[pltpu.VMEM((tk, tq), BF16), pltpu.VMEM((tk, tq), BF16), pltpu.VMEM((VT_ROWS, tq), F32),
                        pltpu.VMEM((VT_ROWS, tq), F32), pltpu.VMEM((8, tq), F32)],
        compiler_params=_cparams(2),
        name="mla_flash",
    )(qt, k, vt)


def _dil_kernel(q_ref, k_ref, v_ref, bias_ref, o_ref, lse_ref, *, L, ta, cr):
    scale = DIL_HEAD_DIM ** -0.5 * LOG2E
    ones = jnp.ones((DIL_KB, DIL_HEAD_DIM), BF16)
    lane = lax.broadcasted_iota(jnp.int32, (DIL_QB, 128), 1)
    base = pl.program_id(2) * ta

    def rows_of(ref, start, n, cs):
        lo = start // cr
        idx = slice(lo, lo + n // cr) if isinstance(start, int) else pl.ds(lo, n // cr)
        return ref[0, idx, 0, :, cs].reshape(n, DIL_HEAD_DIM)

    for t in range(ta // DIL_QB):
        a0 = base + t * DIL_QB
        k0 = jnp.clip(a0 - DIL_SIDE, 0, L - DIL_KB)
        var = (a0 - k0) // DIL_SIDE
        chunks = slice(t * DIL_QB // cr, (t + 1) * DIL_QB // cr)
        lses = []
        for h in range(DIL_HEADS_PER_GROUP):
            cs = slice(h * DIL_HEAD_DIM, (h + 1) * DIL_HEAD_DIM)
            qh = rows_of(q_ref, t * DIL_QB, DIL_QB, cs)
            kh = rows_of(k_ref, k0, DIL_KB, cs)
            vh = rows_of(v_ref, k0, DIL_KB, cs)
            s = _dot_nt(qh, kh) * scale + bias_ref[var, h]
            m = jnp.max(s, axis=-1, keepdims=True)
            p = jnp.exp2(s - m).astype(BF16)
            ov = _dot(p, jnp.concatenate([vh, ones], axis=-1))
            l = ov[:, DIL_HEAD_DIM:]
            o = (ov[:, :DIL_HEAD_DIM] * (1.0 / l)).astype(BF16)
            o_ref[0, chunks, 0, :, cs] = o.reshape(DIL_QB // cr, cr, DIL_HEAD_DIM)
            lses.append(m * (1.0 / LOG2E) + jnp.log(l))
        packed = jnp.where(lane < 32, lses[0], jnp.where(lane < 64, lses[1], jnp.where(lane < 96, lses[2], lses[3])))
        lse_ref[0, chunks, 0, :, :] = packed.reshape(DIL_QB // cr, cr, 128)


def _dilated_group(proj, bias, *, B, S, r, col, name):
    cr = min(PERM_BLOCK // r, DIL_SIDE)
    assert r == 1 or r * cr == PERM_BLOCK
    L = S // r
    per_stream = L // cr
    ta = min(L, 512)
    x5 = proj.reshape(B, per_stream, r, cr, PROJ_W)
    return pl.pallas_call(
        functools.partial(_dil_kernel, L=L, ta=ta, cr=cr),
        grid=(B, r, L // ta),
        in_specs=[
            pl.BlockSpec((1, ta // cr, 1, cr, DIL_W), lambda b, c, a: (b, a, c, 0, col)),
            pl.BlockSpec((1, per_stream, 1, cr, DIL_W), lambda b, c, a: (b, 0, c, 0, col + 1)),
            pl.BlockSpec((1, per_stream, 1, cr, DIL_W), lambda b, c, a: (b, 0, c, 0, col + 2)),
            pl.BlockSpec(bias.shape, lambda b, c, a: (0, 0, 0, 0)),
        ],
        out_specs=[
            pl.BlockSpec((1, ta // cr, 1, cr, DIL_W), lambda b, c, a: (b, a, c, 0, 0)),
            pl.BlockSpec((1, ta // cr, 1, cr, 128), lambda b, c, a: (b, a, c, 0, 0)),
        ],
        out_shape=[
            jax.ShapeDtypeStruct((B, per_stream, r, cr, DIL_W), BF16),
            jax.ShapeDtypeStruct((B, per_stream, r, cr, 128), F32),
        ],
        compiler_params=_cparams(3),
        name=name,
    )(x5, x5, x5, bias)


def _dilated(proj, bias_tabs, *, B, S):
    T = B * S
    ods, lss = [], []
    for g, (_, r) in enumerate(DIL_PAIRS):
        od, ls = _dilated_group(proj, bias_tabs[g], B=B, S=S, r=r, col=(_PROJ_DIL + g * 3 * DIL_W) // DIL_W,
                                name=f"dilated_g{g}")
        ods.append(od.reshape(T, DIL_W))
        lss.append(ls.reshape(T, 128))
    return ods, lss


def _t5_bucket(rel):
    nb = NUM_BUCKETS // 2
    ret = (rel > 0).astype(np.int32) * nb
    n = np.abs(rel)
    max_exact = nb // 2
    large = max_exact + (np.log(np.maximum(n, 1) / max_exact) / np.log(MAX_DISTANCE / max_exact)
                         * (nb - max_exact)).astype(np.int32)
    large = np.minimum(large, nb - 1)
    return (ret + np.where(n < max_exact, n, large)).astype(np.int32)


def _dil_bias_tables(rel_bias):
    qq = np.arange(DIL_QB)[:, None]
    kk = np.arange(DIL_KB)[None, :]
    ext = jnp.concatenate([rel_bias.astype(F32), jnp.full((1, rel_bias.shape[1]), NEG_INF, F32)], axis=0)
    tables = []
    for g, (_, r) in enumerate(DIL_PAIRS):
        idx = []
        for var in range(3):
            j = kk - qq - var * DIL_SIDE
            bucket = _t5_bucket(j * r)
            idx.append(np.where(np.abs(j) <= DIL_SIDE, bucket, NUM_BUCKETS))
        onehot = (np.stack(idx)[..., None] == np.arange(NUM_BUCKETS + 1)).astype(np.float32)
        heads = ext[:, g * DIL_HEADS_PER_GROUP:(g + 1) * DIL_HEADS_PER_GROUP]
        tables.append(jnp.einsum("vqkn,nh->vhqk", onehot, heads * LOG2E, precision=lax.Precision.HIGHEST))
    return tables


def _mem_kv_kernel(m_ref, g_ref, w_ref, o_ref):
    o_ref[...] = _dot(_rms(m_ref[...], g_ref[...]).astype(BF16), w_ref[...]).astype(BF16)


def _mem_kv(mem2, g, w):
    R, D = mem2.shape
    N = w.shape[1]
    return pl.pallas_call(
        _mem_kv_kernel,
        grid=(R // N_MEM,),
        in_specs=[
            pl.BlockSpec((N_MEM, D), lambda i: (i, 0)),
            pl.BlockSpec((1, D), lambda i: (0, 0)),
            pl.BlockSpec((D, N), lambda i: (0, 0)),
        ],
        out_specs=pl.BlockSpec((N_MEM, N), lambda i: (i, 0)),
        out_shape=jax.ShapeDtypeStruct((R, N), BF16),
        compiler_params=_cparams(1),
        name="mem_kv",
    )(mem2, g, w)


def _mem_attn_kernel(q_ref, kv_ref, o_ref):
    scale = X_HEAD_DIM ** -0.5
    hv = X_HEADS * X_HEAD_DIM
    for h in range(X_HEADS):
        cs = slice(h * X_HEAD_DIM, (h + 1) * X_HEAD_DIM)
        s = _dot_nt(q_ref[:, cs], kv_ref[:, cs]) * scale
        m = jnp.max(s, axis=-1, keepdims=True)
        p = jnp.exp(s - m)
        l = jnp.sum(p, axis=-1, keepdims=True)
        o = _dot(p.astype(BF16), kv_ref[:, hv + h * X_HEAD_DIM:hv + (h + 1) * X_HEAD_DIM]) / l
        o_ref[:, cs] = o.astype(BF16)


def _mem_attn(proj, kvm, *, S, tm=512):
    T = proj.shape[0]
    W = X_HEADS * X_HEAD_DIM
    nsb = S // tm
    return pl.pallas_call(
        _mem_attn_kernel,
        grid=(T // tm,),
        in_specs=[
            pl.BlockSpec((tm, W), lambda i: (i, _PROJ_XQ // W)),
            pl.BlockSpec((N_MEM, 2 * W), lambda i: (i // nsb, 0)),
        ],
        out_specs=pl.BlockSpec((tm, W), lambda i: (i, 0)),
        out_shape=jax.ShapeDtypeStruct((T, W), BF16),
        compiler_params=_cparams(1),
        name="mem_attn",
    )(proj, kvm)


def _token_order(pt_ref, od_ref, ls_ref):
    ods, lss = [], []
    for b0 in range(0, od_ref.shape[0], PERM_BLOCK):
        rows = slice(b0, b0 + PERM_BLOCK)
        ods.append(_dot(pt_ref[...], od_ref[rows, :]))
        ls = ls_ref[rows, :]
        hi = ls.astype(BF16)
        rest = ls - hi.astype(F32)
        mid = rest.astype(BF16)
        lo = (rest - mid.astype(F32)).astype(BF16)
        parts = _dot(pt_ref[...], jnp.concatenate([hi, mid, lo], axis=-1))
        lss.append(parts[:, 0:128] + parts[:, 128:256] + parts[:, 256:384])
    return jnp.concatenate(ods, axis=0), jnp.concatenate(lss, axis=0)


def _merge_kernel(gate_ref, omla_ref, od0_ref, od1_ref, od2_ref, ls0_ref, ls1_ref, ls2_ref, omem_ref,
                  wmla_ref, wdil_ref, wmem_ref, pt1_ref, pt2_ref, o_ref):
    od1, ls1 = _token_order(pt1_ref, od1_ref, ls1_ref)
    od2, ls2 = _token_order(pt2_ref, od2_ref, ls2_ref)
    ls = [ls0_ref[...], ls1, ls2]
    mx = jnp.maximum(jnp.maximum(ls[0], ls[1]), ls[2])
    es = [jnp.exp(x - mx) for x in ls]
    inv = 1.0 / (es[0] + es[1] + es[2])
    od = [od0_ref[...].astype(F32), od1, od2]
    parts = []
    for h in range(DIL_HEADS_PER_GROUP):
        cs = slice(h * DIL_HEAD_DIM, (h + 1) * DIL_HEAD_DIM)
        acc = None
        for gi in range(DIL_GROUPS):
            a = (es[gi] * inv)[:, h * 32:h * 32 + 1]
            term = a * od[gi][:, cs]
            acc = term if acc is None else acc + term
        parts.append(acc.astype(BF16))
    o_dil = jnp.concatenate(parts, axis=-1)

    def gate(b):
        return jax.nn.sigmoid(gate_ref[:, b * D_MODEL:(b + 1) * D_MODEL].astype(F32))

    merged = gate(0) * _dot(omla_ref[...], wmla_ref[...])
    merged += gate(1) * _dot(o_dil, wdil_ref[...])
    merged += gate(2) * _dot(omem_ref[...], wmem_ref[...])
    o_ref[...] = merged.astype(BF16)


def _merge(proj, o_mla, ods, lss, o_mem, wmla, wdil, wmem, *, tm=512):
    T = proj.shape[0]
    assert tm % PERM_BLOCK == 0
    row = lambda i: (i, 0)
    const = lambda i: (0, 0)
    wspec = lambda w: pl.BlockSpec(w.shape, const, pipeline_mode=pl.Buffered(1))
    perms_t = [jnp.asarray(_stream_perm(r).T, BF16) for _, r in DIL_PAIRS[1:]]
    return pl.pallas_call(
        _merge_kernel,
        grid=(T // tm,),
        in_specs=[
            pl.BlockSpec((tm, N_BRANCH * D_MODEL), row),
            pl.BlockSpec((tm, o_mla.shape[1]), row),
            pl.BlockSpec((tm, DIL_W), row), pl.BlockSpec((tm, DIL_W), row), pl.BlockSpec((tm, DIL_W), row),
            pl.BlockSpec((tm, 128), row), pl.BlockSpec((tm, 128), row), pl.BlockSpec((tm, 128), row),
            pl.BlockSpec((tm, o_mem.shape[1]), row),
            wspec(wmla), wspec(wdil), wspec(wmem),
            pl.BlockSpec((PERM_BLOCK, PERM_BLOCK), const), pl.BlockSpec((PERM_BLOCK, PERM_BLOCK), const),
        ],
        out_specs=pl.BlockSpec((tm, D_MODEL), row),
        out_shape=jax.ShapeDtypeStruct((T, D_MODEL), BF16),
        compiler_params=_cparams(1),
        name="merge",
    )(proj, o_mla, *ods, *lss, o_mem, wmla, wdil, wmem, *perms_t)


def _out_mlp_kernel(x_ref, mg_ref, wout_ref, gm_ref, wup_ref, wdn_ref, gf_ref, o_ref, h_ref):
    j = pl.program_id(1)

    @pl.when(j == 0)
    def _():
        x1 = x_ref[...] + _dot(mg_ref[...], wout_ref[...])
        o_ref[...] = x1
        h_ref[...] = _rms(x1, gm_ref[...]).astype(BF16)

    u = jnp.maximum(_dot(h_ref[...], wup_ref[...]), 0.0)
    o_ref[...] += _dot((u * u).astype(BF16), wdn_ref[...])

    @pl.when(j == pl.num_programs(1) - 1)
    def _():
        o_ref[...] = _rms(o_ref[...], gf_ref[...])


def _out_mlp(x2, merged, wout, gm, wup, wdn, gf, *, tm=512, tf=1024):
    T, D = x2.shape
    row = lambda i, j: (i, 0)
    const = lambda i, j: (0, 0)
    return pl.pallas_call(
        _out_mlp_kernel,
        grid=(T // tm, D_FF // tf),
        in_specs=[
            pl.BlockSpec((tm, D), row),
            pl.BlockSpec((tm, D), row),
            pl.BlockSpec((D, D), const, pipeline_mode=pl.Buffered(1)),
            pl.BlockSpec((1, D), const),
            pl.BlockSpec((D, tf), lambda i, j: (0, j)),
            pl.BlockSpec((tf, D), lambda i, j: (j, 0)),
            pl.BlockSpec((1, D), const),
        ],
        out_specs=pl.BlockSpec((tm, D), row),
        out_shape=jax.ShapeDtypeStruct((T, D), F32),
        scratch_shapes=[pltpu.VMEM((tm, D), BF16)],
        compiler_params=_cparams(2),
        name="out_mlp",
    )(x2, merged, wout, gm, wup, wdn, gf)


def _rot_half_cols(w):
    half = QK_ROPE // 2
    return jnp.concatenate([-w[..., half:], w[..., :half]], axis=-1)


def _w_in_regroup_kernel(w_ref, o_ref):
    def put(dst, src, n):
        o_ref[:, dst:dst + n] = w_ref[:, src:src + n].astype(BF16)

    src_kr = Q_LORA + KV_LORA
    src_dil = src_kr + QK_ROPE
    src_xq = src_dil + 3 * DIL_GROUPS * DIL_W
    src_gate = src_xq + X_HEADS * X_HEAD_DIM
    put(_PROJ_GATE, src_gate, N_BRANCH * D_MODEL)
    put(_PROJ_XQ, src_xq, X_HEADS * X_HEAD_DIM)
    for g in range(DIL_GROUPS):
        for t in range(3):
            put(_PROJ_DIL + (g * 3 + t) * DIL_W, src_dil + (t * DIL_GROUPS + g) * DIL_W, DIL_W)
    put(_PROJ_CQ, 0, Q_LORA)
    put(_PROJ_CKV, Q_LORA, KV_LORA)
    kr = w_ref[:, src_kr:src_kr + QK_ROPE].astype(BF16)
    z64 = jnp.zeros((kr.shape[0], 64), BF16)
    o_ref[:, _PROJ_KR:_PROJ_KR + 512] = jnp.concatenate([kr, z64, _rot_half_cols(kr), z64] + [z64] * 4, axis=1)


def _w_in_regroup(w_in, *, rows=128):
    D, n_in = w_in.shape
    return pl.pallas_call(
        _w_in_regroup_kernel,
        grid=(D // rows,),
        in_specs=[pl.BlockSpec((rows, n_in), lambda i: (i, 0))],
        out_specs=pl.BlockSpec((rows, PROJ_W), lambda i: (i, 0)),
        out_shape=jax.ShapeDtypeStruct((D, PROJ_W), BF16),
        compiler_params=_cparams(1),
        name="w_in_regroup",
    )(w_in)


def _prep_weights(w_in, w_uq, w_ukv, w_mem_kv, w_b_mla, w_b_dil, w_b_mem, w_out, w_up, w_down):
    w_in_p = _w_in_regroup(w_in)

    wkv = w_ukv.reshape(KV_LORA, MLA_HEADS, QK_NOPE + V_HEAD)
    wk = wkv[..., :QK_NOPE].reshape(KV_LORA, MLA_HEADS * QK_NOPE).astype(BF16)
    wvt = wkv[..., QK_NOPE:].reshape(KV_LORA, MLA_HEADS * V_HEAD).T.astype(BF16)
    return dict(w_in=w_in_p, wqt=w_uq.T.astype(BF16), wk=wk, wvt=wvt, wmkv=w_mem_kv.astype(BF16),
                wmla=w_b_mla.astype(BF16), wdil=w_b_dil.astype(BF16), wmem=w_b_mem.astype(BF16),
                wout=w_out.astype(BF16), wup=w_up.astype(BF16), wdn=w_down.astype(BF16))


def _rope_tables(S):
    half = QK_ROPE // 2
    inv = 1.0 / (ROPE_THETA ** (jnp.arange(half, dtype=F32) / half))
    ang = jnp.arange(S).astype(F32)[:, None] * inv[None, :]
    z = jnp.zeros((S, 64), F32)
    cos, sin = jnp.cos(ang), jnp.sin(ang)
    return jnp.concatenate([cos, cos, z], axis=1), jnp.concatenate([sin, sin, z], axis=1), cos.T, sin.T


def _trunk(x, mem, W, bias_tabs, g_attn, g_qn, g_kvn, g_mem, g_mlp, g_final):
    B, S, D = x.shape
    T = B * S
    x2 = x.reshape(T, D)
    row = lambda g: g.reshape(1, -1).astype(F32)

    proj = _in_proj(x2, row(g_attn), W["w_in"])

    q_scale = (QK_NOPE + QK_ROPE) ** -0.5 * math.log2(math.e)
    qt, k, vt = _mla_proj(proj, _rope_tables(S), row(g_qn), row(g_kvn), W["wqt"], W["wk"], W["wvt"], S=S,
                          q_scale=q_scale)
    o_mla = _mla_flash(qt, k, vt, B=B, S=S)

    ods, lss = _dilated(proj, bias_tabs, B=B, S=S)

    kvm = _mem_kv(mem.reshape(B * N_MEM, D), row(g_mem), W["wmkv"])
    o_mem = _mem_attn(proj, kvm, S=S)

    merged = _merge(proj, o_mla, ods, lss, o_mem, W["wmla"], W["wdil"], W["wmem"])
    y = _out_mlp(x2, merged, W["wout"], row(g_mlp), W["wup"], W["wdn"], row(g_final))
    return y.reshape(B, S, D)


def kernel(x_prompt, x_sample, mem_prompt, mem_sample, rel_bias, g_attn, w_in, g_q_norm, w_uq, g_kv_norm, w_ukv,
           g_mem, w_mem_kv, w_b_mla, w_b_dil, w_b_mem, w_out, g_mlp, w_up, w_down, g_final):
    assert w_in.shape[0] == 1, "single layer"
    W = _prep_weights(w_in[0], w_uq[0], w_ukv[0], w_mem_kv[0], w_b_mla[0], w_b_dil[0], w_b_mem[0], w_out[0],
                      w_up[0], w_down[0])
    bias_tabs = _dil_bias_tables(rel_bias)
    args = (W, bias_tabs, g_attn[0], g_q_norm[0], g_kv_norm[0], g_mem[0], g_mlp[0], g_final)
    return (_trunk(x_prompt, mem_prompt, *args), _trunk(x_sample, mem_sample, *args))
```

```python
import functools
import math

import numpy as np
import jax
import jax.numpy as jnp
from jax import lax
from jax.experimental import pallas as pl
from jax.experimental.pallas import tpu as pltpu

F32 = jnp.float32
BF16 = jnp.bfloat16

D_MODEL = 2048
N_MEM = 256
MLA_HEADS = 12
Q_LORA = 512
KV_LORA = 512
QK_NOPE = 128
QK_ROPE = 64
V_HEAD = 128
ROPE_THETA = 10000.0
DIL_PAIRS = ((128, 1), (512, 4), (2048, 16))
DIL_GROUPS = 3
DIL_HEADS_PER_GROUP = 4
DIL_HEAD_DIM = 128
X_HEADS = 4
X_HEAD_DIM = 256
NUM_BUCKETS = 32
MAX_DISTANCE = 1024
D_FF = 4 * D_MODEL
N_BRANCH = 3
EPS = 1e-6
NEG_INF = -1e30
LOG2E = math.log2(math.e)

DIL_W = DIL_HEADS_PER_GROUP * DIL_HEAD_DIM
DIL_SIDE = 64
DIL_QB = 128
DIL_KB = DIL_QB + 2 * DIL_SIDE
PERM_BLOCK = 256
QK_PAD = 256
VT_ROWS = V_HEAD + 16

_PROJ_GATE = 0
_PROJ_XQ = _PROJ_GATE + N_BRANCH * D_MODEL
_PROJ_DIL = _PROJ_XQ + X_HEADS * X_HEAD_DIM
_PROJ_CQ = _PROJ_DIL + 3 * DIL_GROUPS * DIL_W
_PROJ_CKV = _PROJ_CQ + Q_LORA
_PROJ_KR = _PROJ_CKV + KV_LORA
PROJ_W = _PROJ_KR + 512

VMEM_LIMIT = 52 * 1024 * 1024


def _cparams(n_axes):
    return pltpu.CompilerParams(dimension_semantics=("arbitrary",) * n_axes, vmem_limit_bytes=VMEM_LIMIT)


def _rms(x32, g32):
    return x32 * lax.rsqrt(jnp.mean(x32 * x32, axis=-1, keepdims=True) + EPS) * g32


def _dot(a, b):
    return jnp.dot(a, b, preferred_element_type=F32)


def _dot_nt(a, b):
    return lax.dot_general(a, b, (((1,), (1,)), ((), ())), preferred_element_type=F32)


def _stream_perm(r):
    cr = PERM_BLOCK // r
    p = np.zeros((PERM_BLOCK, PERM_BLOCK), np.float32)
    for c in range(r):
        for a in range(cr):
            p[c * cr + a, r * a + c] = 1.0
    return p


def _in_proj_kernel(x_ref, g_ref, w_ref, p1_ref, p2_ref, o_ref, h_ref, *, tn):
    j = pl.program_id(1)

    @pl.when(j == 0)
    def _():
        for b0 in range(0, x_ref.shape[0], PERM_BLOCK):
            rows = slice(b0, b0 + PERM_BLOCK)
            h = _rms(x_ref[rows, :], g_ref[...]).astype(BF16)
            h_ref[0, rows, :] = h
            h_ref[1, rows, :] = _dot(p1_ref[...], h).astype(BF16)
            h_ref[2, rows, :] = _dot(p2_ref[...], h).astype(BF16)

    half = tn // 2
    for k in range(2):
        lane0 = j * tn + k * half
        ver = (jnp.where(lane0 >= _PROJ_DIL + 3 * DIL_W, 1, 0) + jnp.where(lane0 >= _PROJ_DIL + 6 * DIL_W, 1, 0)
               - jnp.where(lane0 >= _PROJ_CQ, 2, 0))
        o_ref[:, k * half:(k + 1) * half] = _dot(h_ref[ver], w_ref[:, k * half:(k + 1) * half]).astype(BF16)


def _in_proj(x2, g, w, *, tm=1024, tn=1024):
    T, D = x2.shape
    assert tm % PERM_BLOCK == 0 and (3 * DIL_W) % (tn // 2) == 0 and _PROJ_DIL % (tn // 2) == 0
    perms = [jnp.asarray(_stream_perm(r), BF16) for _, r in DIL_PAIRS[1:]]
    const = lambda i, j: (0, 0)
    return pl.pallas_call(
        functools.partial(_in_proj_kernel, tn=tn),
        grid=(T // tm, PROJ_W // tn),
        in_specs=[
            pl.BlockSpec((tm, D), lambda i, j: (i, 0)),
            pl.BlockSpec((1, D), const),
            pl.BlockSpec((D, tn), lambda i, j: (0, j)),
            pl.BlockSpec((PERM_BLOCK, PERM_BLOCK), const),
            pl.BlockSpec((PERM_BLOCK, PERM_BLOCK), const),
        ],
        out_specs=pl.BlockSpec((tm, tn), lambda i, j: (i, j)),
        out_shape=jax.ShapeDtypeStruct((T, PROJ_W), BF16),
        scratch_shapes=[pltpu.VMEM((3, tm, D), BF16)],
        compiler_params=_cparams(2),
        name="in_proj",
    )(x2, g, w, *perms)


def _mla_proj_kernel(cq_ref, ckv_ref, kr_ref, cos_ref, sin_ref, cost_ref, sint_ref, gq_ref, gkv_ref,
                     wqt_ref, wk_ref, wvt_ref, qt_ref, k_ref, vt_ref, *, q_scale):
    half = QK_ROPE // 2
    hd = QK_NOPE + QK_ROPE
    nq_t = _rms(cq_ref[...].astype(F32), gq_ref[...]).T.astype(BF16)
    q_t = _dot(wqt_ref[...], nq_t)
    cos_t = cost_ref[...]
    sin_t = sint_ref[...]
    zeros = jnp.zeros((QK_PAD - hd, q_t.shape[1]), BF16)
    for h in range(MLA_HEADS):
        r0, o0 = h * hd, h * QK_PAD
        x1 = q_t[r0 + QK_NOPE:r0 + QK_NOPE + half]
        x2 = q_t[r0 + QK_NOPE + half:r0 + hd]
        qt_ref[o0:o0 + QK_NOPE, :] = (q_t[r0:r0 + QK_NOPE] * q_scale).astype(BF16)
        qt_ref[o0 + QK_NOPE:o0 + QK_NOPE + half, :] = ((x1 * cos_t - x2 * sin_t) * q_scale).astype(BF16)
        qt_ref[o0 + QK_NOPE + half:o0 + hd, :] = ((x2 * cos_t + x1 * sin_t) * q_scale).astype(BF16)
        qt_ref[o0 + hd:o0 + QK_PAD, :] = zeros

    nkv = _rms(ckv_ref[...].astype(F32), gkv_ref[...])
    v_t = _dot(wvt_ref[...], nkv.T.astype(BF16)).astype(BF16)
    extra = jnp.where(lax.broadcasted_iota(jnp.int32, (VT_ROWS - V_HEAD, v_t.shape[1]), 0) == 0, 1.0, 0.0).astype(BF16)
    for h in range(MLA_HEADS):
        vt_ref[h * VT_ROWS:h * VT_ROWS + V_HEAD, :] = v_t[h * V_HEAD:(h + 1) * V_HEAD]
        vt_ref[h * VT_ROWS + V_HEAD:(h + 1) * VT_ROWS, :] = extra
    kn = _dot(nkv.astype(BF16), wk_ref[...])
    kr = kr_ref[...].astype(F32)
    kpe = (kr[:, 0:128] * cos_ref[...] + kr[:, 128:256] * sin_ref[...]).astype(BF16)
    for h in range(MLA_HEADS):
        o0 = h * QK_PAD
        k_ref[:, o0:o0 + QK_NOPE] = kn[:, h * QK_NOPE:(h + 1) * QK_NOPE].astype(BF16)
        k_ref[:, o0 + QK_NOPE:o0 + QK_PAD] = kpe


def _mla_proj(proj, rope, gq, gkv, wqt, wk, wvt, *, S, q_scale, tm=512):
    T = proj.shape[0]
    nsb = S // tm
    HQ = MLA_HEADS * QK_PAD
    HV = MLA_HEADS * VT_ROWS
    cos_k, sin_k, cos_t, sin_t = rope
    const = lambda i: (0, 0)
    wspec = lambda w: pl.BlockSpec(w.shape, const, pipeline_mode=pl.Buffered(1))
    return pl.pallas_call(
        functools.partial(_mla_proj_kernel, q_scale=q_scale),
        grid=(T // tm,),
        in_specs=[
            pl.BlockSpec((tm, Q_LORA), lambda i: (i, _PROJ_CQ // 512)),
            pl.BlockSpec((tm, KV_LORA), lambda i: (i, _PROJ_CKV // 512)),
            pl.BlockSpec((tm, 512), lambda i: (i, _PROJ_KR // 512)),
            pl.BlockSpec((tm, 128), lambda i: (i % nsb, 0)),
            pl.BlockSpec((tm, 128), lambda i: (i % nsb, 0)),
            pl.BlockSpec((QK_ROPE // 2, tm), lambda i: (0, i % nsb)),
            pl.BlockSpec((QK_ROPE // 2, tm), lambda i: (0, i % nsb)),
            pl.BlockSpec((1, Q_LORA), const),
            pl.BlockSpec((1, KV_LORA), const),
            wspec(wqt), wspec(wk), wspec(wvt),
        ],
        out_specs=[
            pl.BlockSpec((HQ, tm), lambda i: (0, i)),
            pl.BlockSpec((tm, HQ), lambda i: (i, 0)),
            pl.BlockSpec((HV, tm), lambda i: (0, i)),
        ],
        out_shape=[
            jax.ShapeDtypeStruct((HQ, T), BF16),
            jax.ShapeDtypeStruct((T, HQ), BF16),
            jax.ShapeDtypeStruct((HV, T), BF16),
        ],
        compiler_params=_cparams(1),
        name="mla_proj",
    )(proj, proj, proj, cos_k, sin_k, cos_t, sin_t, gq, gkv, wqt, wk, wvt)


FLASH_SLAB = 32
FLASH_QK_ROWS = 512


FLASH_WINDOW = 100.0
FLASH_PV_AFTER = 2
FLASH_EXACT_ROWS = 512


def _zero_like_bits(v):
    u = pltpu.bitcast(v, jnp.uint32)
    return pltpu.bitcast(lax.shift_right_logical(lax.shift_right_logical(u, jnp.uint32(16)), jnp.uint32(16)), F32)


def _mla_flash_kernel(qt_ref, k_ref, vt_ref, o_ref, p0_ref, p1_ref, acc_ref, accd_ref, guess_ref, *, tq, tk, nk, nq):
    p_refs = (p0_ref, p1_ref)
    nc = nq * nk

    def chunk_pos(c):
        qi = c // nk
        return qi, c == qi * nk, pl.multiple_of((c - qi * nk) * tk, tk), pl.multiple_of(qi * tq, tq)

    def pv(c, slot, first=False):
        r0 = chunk_pos(c)[2]
        d = _dot(vt_ref[:, pl.ds(r0, tk)], p_refs[slot][...])
        if first:
            accd_ref[...] = acc_ref[...]
            acc = d
        else:
            acc = acc_ref[...] + d
        acc_ref[...] = acc
        return acc[0:1]

    def qk_exp(c, slot, st, pv_args=None):
        ref, rmax, done_ref, done_rmax = st
        _, first, r0, c0 = chunk_pos(c)
        done_ref = jnp.where(first, ref, done_ref)
        done_rmax = jnp.where(first, rmax, done_rmax)
        ref = jnp.where(first, rmax, ref)
        rmax = jnp.where(first, -jnp.inf, rmax)
        mx = None
        shift = ref
        for n, q0 in enumerate(range(0, tk, FLASH_QK_ROWS)):
            if n == FLASH_PV_AFTER and pv_args is not None:
                shift = ref + _zero_like_bits(pv(*pv_args))
            s = _dot(k_ref[pl.ds(r0 + q0, FLASH_QK_ROWS), :], qt_ref[:, pl.ds(c0, tq)])
            p_refs[slot][q0:q0 + FLASH_QK_ROWS, :] = jnp.exp2(s - shift).astype(BF16)
            part = jnp.max(s.reshape(FLASH_QK_ROWS // FLASH_SLAB, FLASH_SLAB, tq), axis=0)
            mx = part if mx is None else jnp.maximum(mx, part)
        rmax = jnp.maximum(rmax, jnp.max(mx, axis=0, keepdims=True))
        return ref, rmax, done_ref, done_rmax

    def exact_block(qi):
        qt = qt_ref[:, pl.ds(pl.multiple_of(qi * tq, tq), tq)]

        def step(j, carry):
            m, acc = carry
            r0 = pl.multiple_of(j * FLASH_EXACT_ROWS, FLASH_EXACT_ROWS)
            s = _dot(k_ref[pl.ds(r0, FLASH_EXACT_ROWS), :], qt)
            m_new = jnp.maximum(m, jnp.max(s, axis=0, keepdims=True))
            p = jnp.exp2(s - m_new).astype(BF16)
            return m_new, acc * jnp.exp2(m - m_new) + _dot(vt_ref[:, pl.ds(r0, FLASH_EXACT_ROWS)], p)

        init = (jnp.full((1, tq), -jnp.inf, F32), jnp.zeros((VT_ROWS, tq), F32))
        _, acc = lax.fori_loop(0, nk * tk // FLASH_EXACT_ROWS, step, init)
        accd_ref[...] = acc

    def write_block(qi, a_ref):
        acc = a_ref[...]
        o = acc[:V_HEAD] * (1.0 / acc[V_HEAD:V_HEAD + 1])
        o_ref[pl.ds(pl.multiple_of(qi * tq, tq), tq), :] = o.T.astype(BF16)

    def fallback_if_needed(qi, ref_b, rmax_b):
        in_window = jnp.max(jnp.abs(rmax_b - ref_b)) <= FLASH_WINDOW

        @pl.when(jnp.logical_not(in_window))
        def _():
            exact_block(qi)
            write_block(qi, accd_ref)

    @pl.when(jnp.logical_and(pl.program_id(0) == 0, pl.program_id(1) == 0))
    def _():
        guess_ref[...] = jnp.zeros_like(guess_ref)

    acc_ref[...] = jnp.zeros_like(acc_ref)
    guess = guess_ref[0:1, :]
    st = qk_exp(0, 0, (guess, guess, guess, guess))

    def chunks_of_block(i, st, n):
        for j in range(n):
            c = i * nk + j
            st = qk_exp(c + 1, (j + 1) % 2, st, (c, j % 2, j == 0))
        return st

    def block(i, st):
        ref_p, rmax_p = st[2], st[3]
        st = chunks_of_block(i, st, nk)
        write_block(i - 1, accd_ref)
        fallback_if_needed(i - 1, ref_p, rmax_p)
        return st

    st = chunks_of_block(0, st, nk)
    st = lax.fori_loop(1, nq - 1, block, st)
    ref_p, rmax_p = st[2], st[3]
    st = chunks_of_block(nq - 1, st, nk - 1)
    pv(nc - 1, (nk - 1) % 2)
    write_block(nq - 2, accd_ref)
    fallback_if_needed(nq - 2, ref_p, rmax_p)
    write_block(nq - 1, acc_ref)
    fallback_if_needed(nq - 1, st[0], st[1])
    guess_ref[0:1, :] = st[1]


def _mla_flash(qt, k, vt, *, B, S, tq=512, tk=2048):
    T = B * S
    nq, nk = S // tq, S // tk
    assert nk % 2 == 0 and nq >= 3 and tk // FLASH_QK_ROWS > FLASH_PV_AFTER
    return pl.pallas_call(
        functools.partial(_mla_flash_kernel, tq=tq, tk=tk, nk=nk, nq=nq),
        grid=(B, MLA_HEADS),
        in_specs=[
            pl.BlockSpec((QK_PAD, S), lambda b, h: (h, b)),
            pl.BlockSpec((S, QK_PAD), lambda b, h: (b, h)),
            pl.BlockSpec((VT_ROWS, S), lambda b, h: (h, b)),
        ],
        out_specs=pl.BlockSpec((S, V_HEAD), lambda b, h: (b, h)),
        out_shape=jax.ShapeDtypeStruct((T, MLA_HEADS * V_HEAD), BF16),
        scratch_shapes=[pltpu.VMEM((tk, tq), BF16), pltpu.VMEM((tk, tq), BF16), pltpu.VMEM((VT_ROWS, tq), F32),
                        pltpu.VMEM((VT_ROWS, tq), F32), pltpu.VMEM((8, tq), F32)],
        compiler_params=_cparams(2),
        name="mla_flash",
    )(qt, k, vt)


def _dil_kernel(q_ref, k_ref, v_ref, bias_ref, o_ref, lse_ref, *, L, ta, cr):
    scale = DIL_HEAD_DIM ** -0.5 * LOG2E
    ones = jnp.ones((DIL_KB, DIL_HEAD_DIM), BF16)
    lane = lax.broadcasted_iota(jnp.int32, (DIL_QB, 128), 1)
    base = pl.program_id(2) * ta

    def rows_of(ref, start, n, cs):
        lo = start // cr
        idx = slice(lo, lo + n // cr) if isinstance(start, int) else pl.ds(lo, n // cr)
        return ref[0, idx, 0, :, cs].reshape(n, DIL_HEAD_DIM)

    for t in range(ta // DIL_QB):
        a0 = base + t * DIL_QB
        k0 = jnp.clip(a0 - DIL_SIDE, 0, L - DIL_KB)
        var = (a0 - k0) // DIL_SIDE
        chunks = slice(t * DIL_QB // cr, (t + 1) * DIL_QB // cr)
        lses = []
        for h in range(DIL_HEADS_PER_GROUP):
            cs = slice(h * DIL_HEAD_DIM, (h + 1) * DIL_HEAD_DIM)
            qh = rows_of(q_ref, t * DIL_QB, DIL_QB, cs)
            kh = rows_of(k_ref, k0, DIL_KB, cs)
            vh = rows_of(v_ref, k0, DIL_KB, cs)
            s = _dot_nt(qh, kh) * scale + bias_ref[var, h]
            m = jnp.max(s, axis=-1, keepdims=True)
            p = jnp.exp2(s - m).astype(BF16)
            ov = _dot(p, jnp.concatenate([vh, ones], axis=-1))
            l = ov[:, DIL_HEAD_DIM:]
            o = (ov[:, :DIL_HEAD_DIM] * (1.0 / l)).astype(BF16)
            o_ref[0, chunks, 0, :, cs] = o.reshape(DIL_QB // cr, cr, DIL_HEAD_DIM)
            lses.append(m * (1.0 / LOG2E) + jnp.log(l))
        packed = jnp.where(lane < 32, lses[0], jnp.where(lane < 64, lses[1], jnp.where(lane < 96, lses[2], lses[3])))
        lse_ref[0, chunks, 0, :, :] = packed.reshape(DIL_QB // cr, cr, 128)


def _dilated_group(proj, bias, *, B, S, r, col, name):
    cr = min(PERM_BLOCK // r, DIL_SIDE)
    assert r == 1 or r * cr == PERM_BLOCK
    L = S // r
    per_stream = L // cr
    ta = min(L, 1024)
    x5 = proj.reshape(B, per_stream, r, cr, PROJ_W)
    return pl.pallas_call(
        functools.partial(_dil_kernel, L=L, ta=ta, cr=cr),
        grid=(B, r, L // ta),
        in_specs=[
            pl.BlockSpec((1, ta // cr, 1, cr, DIL_W), lambda b, c, a: (b, a, c, 0, col)),
            pl.BlockSpec((1, per_stream, 1, cr, DIL_W), lambda b, c, a: (b, 0, c, 0, col + 1)),
            pl.BlockSpec((1, per_stream, 1, cr, DIL_W), lambda b, c, a: (b, 0, c, 0, col + 2)),
            pl.BlockSpec(bias.shape, lambda b, c, a: (0, 0, 0, 0)),
        ],
        out_specs=[
            pl.BlockSpec((1, ta // cr, 1, cr, DIL_W), lambda b, c, a: (b, a, c, 0, 0)),
            pl.BlockSpec((1, ta // cr, 1, cr, 128), lambda b, c, a: (b, a, c, 0, 0)),
        ],
        out_shape=[
            jax.ShapeDtypeStruct((B, per_stream, r, cr, DIL_W), BF16),
            jax.ShapeDtypeStruct((B, per_stream, r, cr, 128), F32),
        ],
        compiler_params=_cparams(3),
        name=name,
    )(x5, x5, x5, bias)


def _dilated(proj, bias_tabs, *, B, S):
    T = B * S
    ods, lss = [], []
    for g, (_, r) in enumerate(DIL_PAIRS):
        od, ls = _dilated_group(proj, bias_tabs[g], B=B, S=S, r=r, col=(_PROJ_DIL + g * 3 * DIL_W) // DIL_W,
                                name=f"dilated_g{g}")
        ods.append(od.reshape(T, DIL_W))
        lss.append(ls.reshape(T, 128))
    return ods, lss


def _t5_bucket(rel):
    nb = NUM_BUCKETS // 2
    ret = (rel > 0).astype(np.int32) * nb
    n = np.abs(rel)
    max_exact = nb // 2
    large = max_exact + (np.log(np.maximum(n, 1) / max_exact) / np.log(MAX_DISTANCE / max_exact)
                         * (nb - max_exact)).astype(np.int32)
    large = np.minimum(large, nb - 1)
    return (ret + np.where(n < max_exact, n, large)).astype(np.int32)


def _dil_bias_tables(rel_bias):
    qq = np.arange(DIL_QB)[:, None]
    kk = np.arange(DIL_KB)[None, :]
    ext = jnp.concatenate([rel_bias.astype(F32), jnp.full((1, rel_bias.shape[1]), NEG_INF, F32)], axis=0)
    tables = []
    for g, (_, r) in enumerate(DIL_PAIRS):
        idx = []
        for var in range(3):
            j = kk - qq - var * DIL_SIDE
            bucket = _t5_bucket(j * r)
            idx.append(np.where(np.abs(j) <= DIL_SIDE, bucket, NUM_BUCKETS))
        onehot = (np.stack(idx)[..., None] == np.arange(NUM_BUCKETS + 1)).astype(np.float32)
        heads = ext[:, g * DIL_HEADS_PER_GROUP:(g + 1) * DIL_HEADS_PER_GROUP]
        tables.append(jnp.einsum("vqkn,nh->vhqk", onehot, heads * LOG2E, precision=lax.Precision.HIGHEST))
    return tables


def _mem_kv_kernel(m_ref, g_ref, w_ref, o_ref):
    o_ref[...] = _dot(_rms(m_ref[...], g_ref[...]).astype(BF16), w_ref[...]).astype(BF16)


def _mem_kv(mem2, g, w):
    R, D = mem2.shape
    N = w.shape[1]
    return pl.pallas_call(
        _mem_kv_kernel,
        grid=(R // N_MEM,),
        in_specs=[
            pl.BlockSpec((N_MEM, D), lambda i: (i, 0)),
            pl.BlockSpec((1, D), lambda i: (0, 0)),
            pl.BlockSpec((D, N), lambda i: (0, 0)),
        ],
        out_specs=pl.BlockSpec((N_MEM, N), lambda i: (i, 0)),
        out_shape=jax.ShapeDtypeStruct((R, N), BF16),
        compiler_params=_cparams(1),
        name="mem_kv",
    )(mem2, g, w)


def _mem_attn_kernel(q_ref, kv_ref, o_ref):
    scale = X_HEAD_DIM ** -0.5
    hv = X_HEADS * X_HEAD_DIM
    for h in range(X_HEADS):
        cs = slice(h * X_HEAD_DIM, (h + 1) * X_HEAD_DIM)
        s = _dot_nt(q_ref[:, cs], kv_ref[:, cs]) * scale
        m = jnp.max(s, axis=-1, keepdims=True)
        p = jnp.exp(s - m)
        l = jnp.sum(p, axis=-1, keepdims=True)
        o = _dot(p.astype(BF16), kv_ref[:, hv + h * X_HEAD_DIM:hv + (h + 1) * X_HEAD_DIM]) / l
        o_ref[:, cs] = o.astype(BF16)


def _mem_attn(proj, kvm, *, S, tm=512):
    T = proj.shape[0]
    W = X_HEADS * X_HEAD_DIM
    nsb = S // tm
    return pl.pallas_call(
        _mem_attn_kernel,
        grid=(T // tm,),
        in_specs=[
            pl.BlockSpec((tm, W), lambda i: (i, _PROJ_XQ // W)),
            pl.BlockSpec((N_MEM, 2 * W), lambda i: (i // nsb, 0)),
        ],
        out_specs=pl.BlockSpec((tm, W), lambda i: (i, 0)),
        out_shape=jax.ShapeDtypeStruct((T, W), BF16),
        compiler_params=_cparams(1),
        name="mem_attn",
    )(proj, kvm)


def _token_order(pt_ref, od_ref, ls_ref):
    ods, lss = [], []
    for b0 in range(0, od_ref.shape[0], PERM_BLOCK):
        rows = slice(b0, b0 + PERM_BLOCK)
        ods.append(_dot(pt_ref[...], od_ref[rows, :]))
        ls = ls_ref[rows, :]
        hi = ls.astype(BF16)
        rest = ls - hi.astype(F32)
        mid = rest.astype(BF16)
        lo = (rest - mid.astype(F32)).astype(BF16)
        parts = _dot(pt_ref[...], jnp.concatenate([hi, mid, lo], axis=-1))
        lss.append(parts[:, 0:128] + parts[:, 128:256] + parts[:, 256:384])
    return jnp.concatenate(ods, axis=0), jnp.concatenate(lss, axis=0)


def _merge_kernel(gate_ref, omla_ref, od0_ref, od1_ref, od2_ref, ls0_ref, ls1_ref, ls2_ref, omem_ref,
                  wmla_ref, wdil_ref, wmem_ref, pt1_ref, pt2_ref, o_ref):
    od1, ls1 = _token_order(pt1_ref, od1_ref, ls1_ref)
    od2, ls2 = _token_order(pt2_ref, od2_ref, ls2_ref)
    ls = [ls0_ref[...], ls1, ls2]
    mx = jnp.maximum(jnp.maximum(ls[0], ls[1]), ls[2])
    es = [jnp.exp(x - mx) for x in ls]
    inv = 1.0 / (es[0] + es[1] + es[2])
    od = [od0_ref[...].astype(F32), od1, od2]
    parts = []
    for h in range(DIL_HEADS_PER_GROUP):
        cs = slice(h * DIL_HEAD_DIM, (h + 1) * DIL_HEAD_DIM)
        acc = None
        for gi in range(DIL_GROUPS):
            a = (es[gi] * inv)[:, h * 32:h * 32 + 1]
            term = a * od[gi][:, cs]
            acc = term if acc is None else acc + term
        parts.append(acc.astype(BF16))
    o_dil = jnp.concatenate(parts, axis=-1)

    def gate(b):
        return jax.nn.sigmoid(gate_ref[:, b * D_MODEL:(b + 1) * D_MODEL].astype(F32))

    merged = gate(0) * _dot(omla_ref[...], wmla_ref[...])
    merged += gate(1) * _dot(o_dil, wdil_ref[...])
    merged += gate(2) * _dot(omem_ref[...], wmem_ref[...])
    o_ref[...] = merged.astype(BF16)


def _merge(proj, o_mla, ods, lss, o_mem, wmla, wdil, wmem, *, tm=512):
    T = proj.shape[0]
    assert tm % PERM_BLOCK == 0
    row = lambda i: (i, 0)
    const = lambda i: (0, 0)
    wspec = lambda w: pl.BlockSpec(w.shape, const, pipeline_mode=pl.Buffered(1))
    perms_t = [jnp.asarray(_stream_perm(r).T, BF16) for _, r in DIL_PAIRS[1:]]
    return pl.pallas_call(
        _merge_kernel,
        grid=(T // tm,),
        in_specs=[
            pl.BlockSpec((tm, N_BRANCH * D_MODEL), row),
            pl.BlockSpec((tm, o_mla.shape[1]), row),
            pl.BlockSpec((tm, DIL_W), row), pl.BlockSpec((tm, DIL_W), row), pl.BlockSpec((tm, DIL_W), row),
            pl.BlockSpec((tm, 128), row), pl.BlockSpec((tm, 128), row), pl.BlockSpec((tm, 128), row),
            pl.BlockSpec((tm, o_mem.shape[1]), row),
            wspec(wmla), wspec(wdil), wspec(wmem),
            pl.BlockSpec((PERM_BLOCK, PERM_BLOCK), const), pl.BlockSpec((PERM_BLOCK, PERM_BLOCK), const),
        ],
        out_specs=pl.BlockSpec((tm, D_MODEL), row),
        out_shape=jax.ShapeDtypeStruct((T, D_MODEL), BF16),
        compiler_params=_cparams(1),
        name="merge",
    )(proj, o_mla, *ods, *lss, o_mem, wmla, wdil, wmem, *perms_t)


def _out_mlp_kernel(x_ref, mg_ref, wout_ref, gm_ref, wup_ref, wdn_ref, gf_ref, o_ref, h_ref):
    j = pl.program_id(1)

    @pl.when(j == 0)
    def _():
        x1 = x_ref[...] + _dot(mg_ref[...], wout_ref[...])
        o_ref[...] = x1
        h_ref[...] = _rms(x1, gm_ref[...]).astype(BF16)

    u = jnp.maximum(_dot(h_ref[...], wup_ref[...]), 0.0)
    o_ref[...] += _dot((u * u).astype(BF16), wdn_ref[...])

    @pl.when(j == pl.num_programs(1) - 1)
    def _():
        o_ref[...] = _rms(o_ref[...], gf_ref[...])


def _out_mlp(x2, merged, wout, gm, wup, wdn, gf, *, tm=512, tf=1024):
    T, D = x2.shape
    row = lambda i, j: (i, 0)
    const = lambda i, j: (0, 0)
    return pl.pallas_call(
        _out_mlp_kernel,
        grid=(T // tm, D_FF // tf),
        in_specs=[
            pl.BlockSpec((tm, D), row),
            pl.BlockSpec((tm, D), row),
            pl.BlockSpec((D, D), const, pipeline_mode=pl.Buffered(1)),
            pl.BlockSpec((1, D), const),
            pl.BlockSpec((D, tf), lambda i, j: (0, j)),
            pl.BlockSpec((tf, D), lambda i, j: (j, 0)),
            pl.BlockSpec((1, D), const),
        ],
        out_specs=pl.BlockSpec((tm, D), row),
        out_shape=jax.ShapeDtypeStruct((T, D), F32),
        scratch_shapes=[pltpu.VMEM((tm, D), BF16)],
        compiler_params=_cparams(2),
        name="out_mlp",
    )(x2, merged, wout, gm, wup, wdn, gf)


def _rot_half_cols(w):
    half = QK_ROPE // 2
    return jnp.concatenate([-w[..., half:], w[..., :half]], axis=-1)


def _w_in_regroup_kernel(w_ref, o_ref):
    def put(dst, src, n):
        o_ref[:, dst:dst + n] = w_ref[:, src:src + n].astype(BF16)

    src_kr = Q_LORA + KV_LORA
    src_dil = src_kr + QK_ROPE
    src_xq = src_dil + 3 * DIL_GROUPS * DIL_W
    src_gate = src_xq + X_HEADS * X_HEAD_DIM
    put(_PROJ_GATE, src_gate, N_BRANCH * D_MODEL)
    put(_PROJ_XQ, src_xq, X_HEADS * X_HEAD_DIM)
    for g in range(DIL_GROUPS):
        for t in range(3):
            put(_PROJ_DIL + (g * 3 + t) * DIL_W, src_dil + (t * DIL_GROUPS + g) * DIL_W, DIL_W)
    put(_PROJ_CQ, 0, Q_LORA)
    put(_PROJ_CKV, Q_LORA, KV_LORA)
    kr = w_ref[:, src_kr:src_kr + QK_ROPE].astype(BF16)
    z64 = jnp.zeros((kr.shape[0], 64), BF16)
    o_ref[:, _PROJ_KR:_PROJ_KR + 512] = jnp.concatenate([kr, z64, _rot_half_cols(kr), z64] + [z64] * 4, axis=1)


def _w_in_regroup(w_in, *, rows=128):
    _, D, n_in = w_in.shape
    w_in = w_in.reshape(D, n_in)
    return pl.pallas_call(
        _w_in_regroup_kernel,
        grid=(D // rows,),
        in_specs=[pl.BlockSpec((rows, n_in), lambda i: (i, 0))],
        out_specs=pl.BlockSpec((rows, PROJ_W), lambda i: (i, 0)),
        out_shape=jax.ShapeDtypeStruct((D, PROJ_W), BF16),
        compiler_params=_cparams(1),
        name="w_in_regroup",
    )(w_in)


def _prep_weights(w_in, w_uq, w_ukv, w_mem_kv, w_b_mla, w_b_dil, w_b_mem, w_out, w_up, w_down):
    w_in_p = _w_in_regroup(w_in)

    wkv = w_ukv.reshape(KV_LORA, MLA_HEADS, QK_NOPE + V_HEAD)
    wk = wkv[..., :QK_NOPE].reshape(KV_LORA, MLA_HEADS * QK_NOPE).astype(BF16)
    wvt = wkv[..., QK_NOPE:].reshape(KV_LORA, MLA_HEADS * V_HEAD).T.astype(BF16)
    return dict(w_in=w_in_p, wqt=w_uq.T.astype(BF16), wk=wk, wvt=wvt, wmkv=w_mem_kv.astype(BF16),
                wmla=w_b_mla.astype(BF16), wdil=w_b_dil.astype(BF16), wmem=w_b_mem.astype(BF16),
                wout=w_out.astype(BF16), wup=w_up.astype(BF16), wdn=w_down.astype(BF16))


def _rope_tables(S):
    half = QK_ROPE // 2
    inv = 1.0 / (ROPE_THETA ** (jnp.arange(half, dtype=F32) / half))
    ang = jnp.arange(S).astype(F32)[:, None] * inv[None, :]
    z = jnp.zeros((S, 64), F32)
    cos, sin = jnp.cos(ang), jnp.sin(ang)
    return jnp.concatenate([cos, cos, z], axis=1), jnp.concatenate([sin, sin, z], axis=1), cos.T, sin.T


def _trunk(x, mem, W, bias_tabs, g_attn, g_qn, g_kvn, g_mem, g_mlp, g_final):
    B, S, D = x.shape
    T = B * S
    x2 = x.reshape(T, D)
    row = lambda g: g.reshape(1, -1).astype(F32)

    proj = _in_proj(x2, row(g_attn), W["w_in"])

    q_scale = (QK_NOPE + QK_ROPE) ** -0.5 * math.log2(math.e)
    qt, k, vt = _mla_proj(proj, _rope_tables(S), row(g_qn), row(g_kvn), W["wqt"], W["wk"], W["wvt"], S=S,
                          q_scale=q_scale)
    o_mla = _mla_flash(qt, k, vt, B=B, S=S)

    ods, lss = _dilated(proj, bias_tabs, B=B, S=S)

    kvm = _mem_kv(mem.reshape(B * N_MEM, D), row(g_mem), W["wmkv"])
    o_mem = _mem_attn(proj, kvm, S=S)

    merged = _merge(proj, o_mla, ods, lss, o_mem, W["wmla"], W["wdil"], W["wmem"])
    y = _out_mlp(x2, merged, W["wout"], row(g_mlp), W["wup"], W["wdn"], row(g_final))
    return y.reshape(B, S, D)


def kernel(x_prompt, x_sample, mem_prompt, mem_sample, rel_bias, g_attn, w_in, g_q_norm, w_uq, g_kv_norm, w_ukv,
           g_mem, w_mem_kv, w_b_mla, w_b_dil, w_b_mem, w_out, g_mlp, w_up, w_down, g_final):
    assert w_in.shape[0] == 1, "single layer"
    W = _prep_weights(w_in, w_uq[0], w_ukv[0], w_mem_kv[0], w_b_mla[0], w_b_dil[0], w_b_mem[0], w_out[0],
                      w_up[0], w_down[0])
    bias_tabs = _dil_bias_tables(rel_bias)
    args = (W, bias_tabs, g_attn[0], g_q_norm[0], g_kv_norm[0], g_mem[0], g_mlp[0], g_final)
    return (_trunk(x_prompt, mem_prompt, *args), _trunk(x_sample, mem_sample, *args))
```

```python
import functools
import math

import numpy as np
import jax
import jax.numpy as jnp
from jax import lax
from jax.experimental import pallas as pl
from jax.experimental.pallas import tpu as pltpu

F32 = jnp.float32
BF16 = jnp.bfloat16

D_MODEL = 2048
N_MEM = 256
MLA_HEADS = 12
Q_LORA = 512
KV_LORA = 512
QK_NOPE = 128
QK_ROPE = 64
V_HEAD = 128
ROPE_THETA = 10000.0
DIL_PAIRS = ((128, 1), (512, 4), (2048, 16))
DIL_GROUPS = 3
DIL_HEADS_PER_GROUP = 4
DIL_HEAD_DIM = 128
X_HEADS = 4
X_HEAD_DIM = 256
NUM_BUCKETS = 32
MAX_DISTANCE = 1024
D_FF = 4 * D_MODEL
N_BRANCH = 3
EPS = 1e-6
NEG_INF = -1e30
LOG2E = math.log2(math.e)

DIL_W = DIL_HEADS_PER_GROUP * DIL_HEAD_DIM
DIL_SIDE = 64
DIL_QB = 128
DIL_KB = DIL_QB + 2 * DIL_SIDE
PERM_BLOCK = 256
QK_PAD = 256
VT_ROWS = V_HEAD + 16

_PROJ_GATE = 0
_PROJ_XQ = _PROJ_GATE + N_BRANCH * D_MODEL
_PROJ_DIL = _PROJ_XQ + X_HEADS * X_HEAD_DIM
_PROJ_CQ = _PROJ_DIL + 3 * DIL_GROUPS * DIL_W
_PROJ_CKV = _PROJ_CQ + Q_LORA
_PROJ_KR = _PROJ_CKV + KV_LORA
PROJ_W = _PROJ_KR + 512

VMEM_LIMIT = 52 * 1024 * 1024


def _cparams(n_axes):
    return pltpu.CompilerParams(dimension_semantics=("arbitrary",) * n_axes, vmem_limit_bytes=VMEM_LIMIT)


def _rms(x32, g32):
    return x32 * lax.rsqrt(jnp.mean(x32 * x32, axis=-1, keepdims=True) + EPS) * g32


def _dot(a, b):
    return jnp.dot(a, b, preferred_element_type=F32)


def _dot_nt(a, b):
    return lax.dot_general(a, b, (((1,), (1,)), ((), ())), preferred_element_type=F32)


def _stream_perm(r):
    cr = PERM_BLOCK // r
    p = np.zeros((PERM_BLOCK, PERM_BLOCK), np.float32)
    for c in range(r):
        for a in range(cr):
            p[c * cr + a, r * a + c] = 1.0
    return p


def _in_proj_kernel(x_ref, g_ref, w_ref, p1_ref, p2_ref, o_ref, h_ref, *, tn):
    j = pl.program_id(1)

    @pl.when(j == 0)
    def _():
        for b0 in range(0, x_ref.shape[0], PERM_BLOCK):
            rows = slice(b0, b0 + PERM_BLOCK)
            h = _rms(x_ref[rows, :], g_ref[...]).astype(BF16)
            h_ref[0, rows, :] = h
            h_ref[1, rows, :] = _dot(p1_ref[...], h).astype(BF16)
            h_ref[2, rows, :] = _dot(p2_ref[...], h).astype(BF16)

    half = tn // 2
    for k in range(2):
        lane0 = j * tn + k * half
        ver = (jnp.where(lane0 >= _PROJ_DIL + 3 * DIL_W, 1, 0) + jnp.where(lane0 >= _PROJ_DIL + 6 * DIL_W, 1, 0)
               - jnp.where(lane0 >= _PROJ_CQ, 2, 0))
        o_ref[:, k * half:(k + 1) * half] = _dot(h_ref[ver], w_ref[:, k * half:(k + 1) * half]).astype(BF16)


def _in_proj(x2, g, w, *, tm=1024, tn=1024):
    T, D = x2.shape
    assert tm % PERM_BLOCK == 0 and (3 * DIL_W) % (tn // 2) == 0 and _PROJ_DIL % (tn // 2) == 0
    perms = [jnp.asarray(_stream_perm(r), BF16) for _, r in DIL_PAIRS[1:]]
    const = lambda i, j: (0, 0)
    return pl.pallas_call(
        functools.partial(_in_proj_kernel, tn=tn),
        grid=(T // tm, PROJ_W // tn),
        in_specs=[
            pl.BlockSpec((tm, D), lambda i, j: (i, 0)),
            pl.BlockSpec((1, D), const),
            pl.BlockSpec((D, tn), lambda i, j: (0, j)),
            pl.BlockSpec((PERM_BLOCK, PERM_BLOCK), const),
            pl.BlockSpec((PERM_BLOCK, PERM_BLOCK), const),
        ],
        out_specs=pl.BlockSpec((tm, tn), lambda i, j: (i, j)),
        out_shape=jax.ShapeDtypeStruct((T, PROJ_W), BF16),
        scratch_shapes=[pltpu.VMEM((3, tm, D), BF16)],
        compiler_params=_cparams(2),
        name="in_proj",
    )(x2, g, w, *perms)


def _mla_proj_kernel(cq_ref, ckv_ref, kr_ref, cos_ref, sin_ref, cost_ref, sint_ref, gq_ref, gkv_ref,
                     wqt_ref, wk_ref, wvt_ref, qt_ref, k_ref, vt_ref, *, q_scale):
    half = QK_ROPE // 2
    hd = QK_NOPE + QK_ROPE
    nq_t = _rms(cq_ref[...].astype(F32), gq_ref[...]).T.astype(BF16)
    q_t = _dot(wqt_ref[...], nq_t)
    cos_t = cost_ref[...]
    sin_t = sint_ref[...]
    zeros = jnp.zeros((QK_PAD - hd, q_t.shape[1]), BF16)
    for h in range(MLA_HEADS):
        r0, o0 = h * hd, h * QK_PAD
        x1 = q_t[r0 + QK_NOPE:r0 + QK_NOPE + half]
        x2 = q_t[r0 + QK_NOPE + half:r0 + hd]
        qt_ref[o0:o0 + QK_NOPE, :] = (q_t[r0:r0 + QK_NOPE] * q_scale).astype(BF16)
        qt_ref[o0 + QK_NOPE:o0 + QK_NOPE + half, :] = ((x1 * cos_t - x2 * sin_t) * q_scale).astype(BF16)
        qt_ref[o0 + QK_NOPE + half:o0 + hd, :] = ((x2 * cos_t + x1 * sin_t) * q_scale).astype(BF16)
        qt_ref[o0 + hd:o0 + QK_PAD, :] = zeros

    nkv = _rms(ckv_ref[...].astype(F32), gkv_ref[...])
    v_t = _dot(wvt_ref[...], nkv.T.astype(BF16)).astype(BF16)
    extra = jnp.where(lax.broadcasted_iota(jnp.int32, (VT_ROWS - V_HEAD, v_t.shape[1]), 0) == 0, 1.0, 0.0).astype(BF16)
    for h in range(MLA_HEADS):
        vt_ref[h * VT_ROWS:h * VT_ROWS + V_HEAD, :] = v_t[h * V_HEAD:(h + 1) * V_HEAD]
        vt_ref[h * VT_ROWS + V_HEAD:(h + 1) * VT_ROWS, :] = extra
    kn = _dot(nkv.astype(BF16), wk_ref[...])
    kr = kr_ref[...].astype(F32)
    kpe = (kr[:, 0:128] * cos_ref[...] + kr[:, 128:256] * sin_ref[...]).astype(BF16)
    for h in range(MLA_HEADS):
        o0 = h * QK_PAD
        k_ref[:, o0:o0 + QK_NOPE] = kn[:, h * QK_NOPE:(h + 1) * QK_NOPE].astype(BF16)
        k_ref[:, o0 + QK_NOPE:o0 + QK_PAD] = kpe


def _mla_proj(proj, rope, gq, gkv, wqt, wk, wvt, *, S, q_scale, tm=512):
    T = proj.shape[0]
    nsb = S // tm
    HQ = MLA_HEADS * QK_PAD
    HV = MLA_HEADS * VT_ROWS
    cos_k, sin_k, cos_t, sin_t = rope
    const = lambda i: (0, 0)
    wspec = lambda w: pl.BlockSpec(w.shape, const, pipeline_mode=pl.Buffered(1))
    return pl.pallas_call(
        functools.partial(_mla_proj_kernel, q_scale=q_scale),
        grid=(T // tm,),
        in_specs=[
            pl.BlockSpec((tm, Q_LORA), lambda i: (i, _PROJ_CQ // 512)),
            pl.BlockSpec((tm, KV_LORA), lambda i: (i, _PROJ_CKV // 512)),
            pl.BlockSpec((tm, 512), lambda i: (i, _PROJ_KR // 512)),
            pl.BlockSpec((tm, 128), lambda i: (i % nsb, 0)),
            pl.BlockSpec((tm, 128), lambda i: (i % nsb, 0)),
            pl.BlockSpec((QK_ROPE // 2, tm), lambda i: (0, i % nsb)),
            pl.BlockSpec((QK_ROPE // 2, tm), lambda i: (0, i % nsb)),
            pl.BlockSpec((1, Q_LORA), const),
            pl.BlockSpec((1, KV_LORA), const),
            wspec(wqt), wspec(wk), wspec(wvt),
        ],
        out_specs=[
            pl.BlockSpec((HQ, tm), lambda i: (0, i)),
            pl.BlockSpec((tm, HQ), lambda i: (i, 0)),
            pl.BlockSpec((HV, tm), lambda i: (0, i)),
        ],
        out_shape=[
            jax.ShapeDtypeStruct((HQ, T), BF16),
            jax.ShapeDtypeStruct((T, HQ), BF16),
            jax.ShapeDtypeStruct((HV, T), BF16),
        ],
        compiler_params=_cparams(1),
        name="mla_proj",
    )(proj, proj, proj, cos_k, sin_k, cos_t, sin_t, gq, gkv, wqt, wk, wvt)


FLASH_SLAB = 32
FLASH_QK_ROWS = 512


FLASH_WINDOW = 100.0
FLASH_PV_AFTER = 2
FLASH_EXACT_ROWS = 512


def _zero_like_bits(v):
    u = pltpu.bitcast(v, jnp.uint32)
    return pltpu.bitcast(lax.shift_right_logical(lax.shift_right_logical(u, jnp.uint32(16)), jnp.uint32(16)), F32)


def _mla_flash_kernel(qt_ref, k_ref, vt_ref, o_ref, p0_ref, p1_ref, acc_ref, accd_ref, guess_ref, *, tq, tk, nk, nq):
    p_refs = (p0_ref, p1_ref)
    nc = nq * nk

    def chunk_pos(c):
        qi = c // nk
        return qi, c == qi * nk, pl.multiple_of((c - qi * nk) * tk, tk), pl.multiple_of(qi * tq, tq)

    def pv(c, slot, first=False):
        r0 = chunk_pos(c)[2]
        d = _dot(vt_ref[:, pl.ds(r0, tk)], p_refs[slot][...])
        if first:
            accd_ref[...] = acc_ref[...]
            acc = d
        else:
            acc = acc_ref[...] + d
        acc_ref[...] = acc
        return acc[0:1]

    def qk_exp(c, slot, st, pv_args=None):
        ref, rmax, done_ref, done_rmax = st
        _, first, r0, c0 = chunk_pos(c)
        done_ref = jnp.where(first, ref, done_ref)
        done_rmax = jnp.where(first, rmax, done_rmax)
        ref = jnp.where(first, rmax, ref)
        rmax = jnp.where(first, -jnp.inf, rmax)
        mx = None
        shift = ref
        for n, q0 in enumerate(range(0, tk, FLASH_QK_ROWS)):
            if n == FLASH_PV_AFTER and pv_args is not None:
                shift = ref + _zero_like_bits(pv(*pv_args))
            s = _dot(k_ref[pl.ds(r0 + q0, FLASH_QK_ROWS), :], qt_ref[:, pl.ds(c0, tq)])
            p_refs[slot][q0:q0 + FLASH_QK_ROWS, :] = jnp.exp2(s - shift).astype(BF16)
            part = jnp.max(s.reshape(FLASH_QK_ROWS // FLASH_SLAB, FLASH_SLAB, tq), axis=0)
            mx = part if mx is None else jnp.maximum(mx, part)
        rmax = jnp.maximum(rmax, jnp.max(mx, axis=0, keepdims=True))
        return ref, rmax, done_ref, done_rmax

    def exact_block(qi):
        qt = qt_ref[:, pl.ds(pl.multiple_of(qi * tq, tq), tq)]

        def step(j, carry):
            m, acc = carry
            r0 = pl.multiple_of(j * FLASH_EXACT_ROWS, FLASH_EXACT_ROWS)
            s = _dot(k_ref[pl.ds(r0, FLASH_EXACT_ROWS), :], qt)
            m_new = jnp.maximum(m, jnp.max(s, axis=0, keepdims=True))
            p = jnp.exp2(s - m_new).astype(BF16)
            return m_new, acc * jnp.exp2(m - m_new) + _dot(vt_ref[:, pl.ds(r0, FLASH_EXACT_ROWS)], p)

        init = (jnp.full((1, tq), -jnp.inf, F32), jnp.zeros((VT_ROWS, tq), F32))
        _, acc = lax.fori_loop(0, nk * tk // FLASH_EXACT_ROWS, step, init)
        accd_ref[...] = acc

    def write_block(qi, a_ref):
        acc = a_ref[...]
        o = acc[:V_HEAD] * (1.0 / acc[V_HEAD:V_HEAD + 1])
        o_ref[pl.ds(pl.multiple_of(qi * tq, tq), tq), :] = o.T.astype(BF16)

    def fallback_if_needed(qi, ref_b, rmax_b):
        in_window = jnp.max(jnp.abs(rmax_b - ref_b)) <= FLASH_WINDOW

        @pl.when(jnp.logical_not(in_window))
        def _():
            exact_block(qi)
            write_block(qi, accd_ref)

    @pl.when(jnp.logical_and(pl.program_id(0) == 0, pl.program_id(1) == 0))
    def _():
        guess_ref[...] = jnp.zeros_like(guess_ref)

    acc_ref[...] = jnp.zeros_like(acc_ref)
    guess = guess_ref[0:1, :]
    st = qk_exp(0, 0, (guess, guess, guess, guess))

    def chunks_of_block(i, st, n):
        for j in range(n):
            c = i * nk + j
            st = qk_exp(c + 1, (j + 1) % 2, st, (c, j % 2, j == 0))
        return st

    def block(i, st):
        ref_p, rmax_p = st[2], st[3]
        st = chunks_of_block(i, st, nk)
        write_block(i - 1, accd_ref)
        fallback_if_needed(i - 1, ref_p, rmax_p)
        return st

    st = chunks_of_block(0, st, nk)
    st = lax.fori_loop(1, nq - 1, block, st)
    ref_p, rmax_p = st[2], st[3]
    st = chunks_of_block(nq - 1, st, nk - 1)
    pv(nc - 1, (nk - 1) % 2)
    write_block(nq - 2, accd_ref)
    fallback_if_needed(nq - 2, ref_p, rmax_p)
    write_block(nq - 1, acc_ref)
    fallback_if_needed(nq - 1, st[0], st[1])
    guess_ref[0:1, :] = st[1]


def _mla_flash(qt, k, vt, *, B, S, tq=512, tk=2048):
    T = B * S
    nq, nk = S // tq, S // tk
    assert nk % 2 == 0 and nq >= 3 and tk // FLASH_QK_ROWS > FLASH_PV_AFTER
    return pl.pallas_call(
        functools.partial(_mla_flash_kernel, tq=tq, tk=tk, nk=nk, nq=nq),
        grid=(B, MLA_HEADS),
        in_specs=[
            pl.BlockSpec((QK_PAD, S), lambda b, h: (h, b)),
            pl.BlockSpec((S, QK_PAD), lambda b, h: (b, h)),
            pl.BlockSpec((VT_ROWS, S), lambda b, h: (h, b)),
        ],
        out_specs=pl.BlockSpec((S, V_HEAD), lambda b, h: (b, h)),
        out_shape=jax.ShapeDtypeStruct((T, MLA_HEADS * V_HEAD), BF16),
        scratch_shapes=[pltpu.VMEM((tk, tq), BF16), pltpu.VMEM((tk, tq), BF16), pltpu.VMEM((VT_ROWS, tq), F32),
                        pltpu.VMEM((VT_ROWS, tq), F32), pltpu.VMEM((8, tq), F32)],
        compiler_params=_cparams(2),
        name="mla_flash",
    )(qt, k, vt)


def _dil_kernel(q_ref, k_ref, v_ref, bias_ref, o_ref, lse_ref, *, L, ta, cr):
    scale = DIL_HEAD_DIM ** -0.5 * LOG2E
    ones = jnp.ones((DIL_KB, DIL_HEAD_DIM), BF16)
    lane = lax.broadcasted_iota(jnp.int32, (DIL_QB, 128), 1)
    base = pl.program_id(2) * ta

    def rows_of(ref, start, n, cs):
        lo = start // cr
        idx = slice(lo, lo + n // cr) if isinstance(start, int) else pl.ds(lo, n // cr)
        return ref[0, idx, 0, :, cs].reshape(n, DIL_HEAD_DIM)

    for t in range(ta // DIL_QB):
        a0 = base + t * DIL_QB
        k0 = jnp.clip(a0 - DIL_SIDE, 0, L - DIL_KB)
        var = (a0 - k0) // DIL_SIDE
        chunks = slice(t * DIL_QB // cr, (t + 1) * DIL_QB // cr)
        lses = []
        for h in range(DIL_HEADS_PER_GROUP):
            cs = slice(h * DIL_HEAD_DIM, (h + 1) * DIL_HEAD_DIM)
            qh = rows_of(q_ref, t * DIL_QB, DIL_QB, cs)
            kh = rows_of(k_ref, k0, DIL_KB, cs)
            vh = rows_of(v_ref, k0, DIL_KB, cs)
            s = _dot_nt(qh, kh) * scale + bias_ref[var, h]
            m = jnp.max(s, axis=-1, keepdims=True)
            p = jnp.exp2(s - m).astype(BF16)
            ov = _dot(p, jnp.concatenate([vh, ones], axis=-1))
            l = ov[:, DIL_HEAD_DIM:]
            o = (ov[:, :DIL_HEAD_DIM] * (1.0 / l)).astype(BF16)
            o_ref[0, chunks, 0, :, cs] = o.reshape(DIL_QB // cr, cr, DIL_HEAD_DIM)
            lses.append(m * (1.0 / LOG2E) + jnp.log(l))
        packed = jnp.where(lane < 32, lses[0], jnp.where(lane < 64, lses[1], jnp.where(lane < 96, lses[2], lses[3])))
        lse_ref[0, chunks, 0, :, :] = packed.reshape(DIL_QB // cr, cr, 128)


def _dilated_group(proj, bias, *, B, S, r, col, name):
    cr = min(PERM_BLOCK // r, DIL_SIDE)
    assert r == 1 or r * cr == PERM_BLOCK
    L = S // r
    per_stream = L // cr
    ta = min(L, 1024)
    x5 = proj.reshape(B, per_stream, r, cr, PROJ_W)
    return pl.pallas_call(
        functools.partial(_dil_kernel, L=L, ta=ta, cr=cr),
        grid=(B, r, L // ta),
        in_specs=[
            pl.BlockSpec((1, ta // cr, 1, cr, DIL_W), lambda b, c, a: (b, a, c, 0, col)),
            pl.BlockSpec((1, per_stream, 1, cr, DIL_W), lambda b, c, a: (b, 0, c, 0, col + 1)),
            pl.BlockSpec((1, per_stream, 1, cr, DIL_W), lambda b, c, a: (b, 0, c, 0, col + 2)),
            pl.BlockSpec(bias.shape, lambda b, c, a: (0, 0, 0, 0)),
        ],
        out_specs=[
            pl.BlockSpec((1, ta // cr, 1, cr, DIL_W), lambda b, c, a: (b, a, c, 0, 0)),
            pl.BlockSpec((1, ta // cr, 1, cr, 128), lambda b, c, a: (b, a, c, 0, 0)),
        ],
        out_shape=[
            jax.ShapeDtypeStruct((B, per_stream, r, cr, DIL_W), BF16),
            jax.ShapeDtypeStruct((B, per_stream, r, cr, 128), F32),
        ],
        compiler_params=_cparams(3),
        name=name,
    )(x5, x5, x5, bias)


def _dilated(proj, bias_tabs, *, B, S):
    T = B * S
    ods, lss = [], []
    for g, (_, r) in enumerate(DIL_PAIRS):
        od, ls = _dilated_group(proj, bias_tabs[g], B=B, S=S, r=r, col=(_PROJ_DIL + g * 3 * DIL_W) // DIL_W,
                                name=f"dilated_g{g}")
        ods.append(od.reshape(T, DIL_W))
        lss.append(ls.reshape(T, 128))
    return ods, lss


def _t5_bucket(rel):
    nb = NUM_BUCKETS // 2
    ret = (rel > 0).astype(np.int32) * nb
    n = np.abs(rel)
    max_exact = nb // 2
    large = max_exact + (np.log(np.maximum(n, 1) / max_exact) / np.log(MAX_DISTANCE / max_exact)
                         * (nb - max_exact)).astype(np.int32)
    large = np.minimum(large, nb - 1)
    return (ret + np.where(n < max_exact, n, large)).astype(np.int32)


def _dil_bias_tables(rel_bias):
    qq = np.arange(DIL_QB)[:, None]
    kk = np.arange(DIL_KB)[None, :]
    ext = jnp.concatenate([rel_bias.astype(F32), jnp.full((1, rel_bias.shape[1]), NEG_INF, F32)], axis=0)
    tables = []
    for g, (_, r) in enumerate(DIL_PAIRS):
        idx = []
        for var in range(3):
            j = kk - qq - var * DIL_SIDE
            bucket = _t5_bucket(j * r)
            idx.append(np.where(np.abs(j) <= DIL_SIDE, bucket, NUM_BUCKETS))
        onehot = (np.stack(idx)[..., None] == np.arange(NUM_BUCKETS + 1)).astype(np.float32)
        heads = ext[:, g * DIL_HEADS_PER_GROUP:(g + 1) * DIL_HEADS_PER_GROUP]
        tables.append(jnp.einsum("vqkn,nh->vhqk", onehot, heads * LOG2E, precision=lax.Precision.HIGHEST))
    return tables


def _mem_kv_kernel(m_ref, g_ref, w_ref, o_ref):
    o_ref[...] = _dot(_rms(m_ref[...], g_ref[...]).astype(BF16), w_ref[...]).astype(BF16)


def _mem_kv(mem2, g, w):
    R, D = mem2.shape
    N = w.shape[1]
    return pl.pallas_call(
        _mem_kv_kernel,
        grid=(R // N_MEM,),
        in_specs=[
            pl.BlockSpec((N_MEM, D), lambda i: (i, 0)),
            pl.BlockSpec((1, D), lambda i: (0, 0)),
            pl.BlockSpec((D, N), lambda i: (0, 0)),
        ],
        out_specs=pl.BlockSpec((N_MEM, N), lambda i: (i, 0)),
        out_shape=jax.ShapeDtypeStruct((R, N), BF16),
        compiler_params=_cparams(1),
        name="mem_kv",
    )(mem2, g, w)


def _mem_attention(q_ref, kv_ref):
    scale = X_HEAD_DIM ** -0.5
    hv = X_HEADS * X_HEAD_DIM
    outs = []
    for h in range(X_HEADS):
        cs = slice(h * X_HEAD_DIM, (h + 1) * X_HEAD_DIM)
        s = _dot_nt(q_ref[:, cs], kv_ref[:, cs]) * scale
        m = jnp.max(s, axis=-1, keepdims=True)
        p = jnp.exp(s - m)
        l = jnp.sum(p, axis=-1, keepdims=True)
        o = _dot(p.astype(BF16), kv_ref[:, hv + h * X_HEAD_DIM:hv + (h + 1) * X_HEAD_DIM]) / l
        outs.append(o.astype(BF16))
    return jnp.concatenate(outs, axis=-1)


def _token_order(pt_ref, od_ref, ls_ref):
    ods, lss = [], []
    for b0 in range(0, od_ref.shape[0], PERM_BLOCK):
        rows = slice(b0, b0 + PERM_BLOCK)
        ods.append(_dot(pt_ref[...], od_ref[rows, :]))
        ls = ls_ref[rows, :]
        hi = ls.astype(BF16)
        rest = ls - hi.astype(F32)
        mid = rest.astype(BF16)
        lo = (rest - mid.astype(F32)).astype(BF16)
        parts = _dot(pt_ref[...], jnp.concatenate([hi, mid, lo], axis=-1))
        lss.append(parts[:, 0:128] + parts[:, 128:256] + parts[:, 256:384])
    return jnp.concatenate(ods, axis=0), jnp.concatenate(lss, axis=0)


def _merge_kernel(gate_ref, omla_ref, od0_ref, od1_ref, od2_ref, ls0_ref, ls1_ref, ls2_ref, xq_ref, kvm_ref,
                  wmla_ref, wdil_ref, wmem_ref, pt1_ref, pt2_ref, o_ref):
    od1, ls1 = _token_order(pt1_ref, od1_ref, ls1_ref)
    od2, ls2 = _token_order(pt2_ref, od2_ref, ls2_ref)
    ls = [ls0_ref[...], ls1, ls2]
    mx = jnp.maximum(jnp.maximum(ls[0], ls[1]), ls[2])
    es = [jnp.exp(x - mx) for x in ls]
    inv = 1.0 / (es[0] + es[1] + es[2])
    od = [od0_ref[...].astype(F32), od1, od2]
    parts = []
    for h in range(DIL_HEADS_PER_GROUP):
        cs = slice(h * DIL_HEAD_DIM, (h + 1) * DIL_HEAD_DIM)
        acc = None
        for gi in range(DIL_GROUPS):
            a = (es[gi] * inv)[:, h * 32:h * 32 + 1]
            term = a * od[gi][:, cs]
            acc = term if acc is None else acc + term
        parts.append(acc.astype(BF16))
    o_dil = jnp.concatenate(parts, axis=-1)

    def gate(b):
        return jax.nn.sigmoid(gate_ref[:, b * D_MODEL:(b + 1) * D_MODEL].astype(F32))

    merged = gate(0) * _dot(omla_ref[...], wmla_ref[...])
    merged += gate(1) * _dot(o_dil, wdil_ref[...])
    merged += gate(2) * _dot(_mem_attention(xq_ref, kvm_ref), wmem_ref[...])
    o_ref[...] = merged.astype(BF16)


def _merge(proj, o_mla, ods, lss, kvm, wmla, wdil, wmem, *, S, tm=512):
    T = proj.shape[0]
    assert tm % PERM_BLOCK == 0 and S % tm == 0
    nsb = S // tm
    xw = X_HEADS * X_HEAD_DIM
    row = lambda i: (i, 0)
    const = lambda i: (0, 0)
    wspec = lambda w: pl.BlockSpec(w.shape, const, pipeline_mode=pl.Buffered(1))
    perms_t = [jnp.asarray(_stream_perm(r).T, BF16) for _, r in DIL_PAIRS[1:]]
    return pl.pallas_call(
        _merge_kernel,
        grid=(T // tm,),
        in_specs=[
            pl.BlockSpec((tm, N_BRANCH * D_MODEL), row),
            pl.BlockSpec((tm, o_mla.shape[1]), row),
            pl.BlockSpec((tm, DIL_W), row), pl.BlockSpec((tm, DIL_W), row), pl.BlockSpec((tm, DIL_W), row),
            pl.BlockSpec((tm, 128), row), pl.BlockSpec((tm, 128), row), pl.BlockSpec((tm, 128), row),
            pl.BlockSpec((tm, xw), lambda i: (i, _PROJ_XQ // xw)),
            pl.BlockSpec((N_MEM, 2 * xw), lambda i: (i // nsb, 0)),
            wspec(wmla), wspec(wdil), wspec(wmem),
            pl.BlockSpec((PERM_BLOCK, PERM_BLOCK), const), pl.BlockSpec((PERM_BLOCK, PERM_BLOCK), const),
        ],
        out_specs=pl.BlockSpec((tm, D_MODEL), row),
        out_shape=jax.ShapeDtypeStruct((T, D_MODEL), BF16),
        compiler_params=_cparams(1),
        name="merge",
    )(proj, o_mla, *ods, *lss, proj, kvm, wmla, wdil, wmem, *perms_t)


def _out_mlp_kernel(x_ref, mg_ref, wout_ref, gm_ref, wup_ref, wdn_ref, gf_ref, o_ref, h_ref):
    j = pl.program_id(1)

    @pl.when(j == 0)
    def _():
        x1 = x_ref[...] + _dot(mg_ref[...], wout_ref[...])
        o_ref[...] = x1
        h_ref[...] = _rms(x1, gm_ref[...]).astype(BF16)

    u = jnp.maximum(_dot(h_ref[...], wup_ref[...]), 0.0)
    o_ref[...] += _dot((u * u).astype(BF16), wdn_ref[...])

    @pl.when(j == pl.num_programs(1) - 1)
    def _():
        o_ref[...] = _rms(o_ref[...], gf_ref[...])


def _out_mlp(x2, merged, wout, gm, wup, wdn, gf, *, tm=512, tf=1024):
    T, D = x2.shape
    row = lambda i, j: (i, 0)
    const = lambda i, j: (0, 0)
    return pl.pallas_call(
        _out_mlp_kernel,
        grid=(T // tm, D_FF // tf),
        in_specs=[
            pl.BlockSpec((tm, D), row),
            pl.BlockSpec((tm, D), row),
            pl.BlockSpec((D, D), const, pipeline_mode=pl.Buffered(1)),
            pl.BlockSpec((1, D), const),
            pl.BlockSpec((D, tf), lambda i, j: (0, j)),
            pl.BlockSpec((tf, D), lambda i, j: (j, 0)),
            pl.BlockSpec((1, D), const),
        ],
        out_specs=pl.BlockSpec((tm, D), row),
        out_shape=jax.ShapeDtypeStruct((T, D), F32),
        scratch_shapes=[pltpu.VMEM((tm, D), BF16)],
        compiler_params=_cparams(2),
        name="out_mlp",
    )(x2, merged, wout, gm, wup, wdn, gf)


def _rot_half_cols(w):
    half = QK_ROPE // 2
    return jnp.concatenate([-w[..., half:], w[..., :half]], axis=-1)


def _w_in_regroup_kernel(w_ref, o_ref):
    def put(dst, src, n):
        o_ref[:, dst:dst + n] = w_ref[:, src:src + n].astype(BF16)

    src_kr = Q_LORA + KV_LORA
    src_dil = src_kr + QK_ROPE
    src_xq = src_dil + 3 * DIL_GROUPS * DIL_W
    src_gate = src_xq + X_HEADS * X_HEAD_DIM
    put(_PROJ_GATE, src_gate, N_BRANCH * D_MODEL)
    put(_PROJ_XQ, src_xq, X_HEADS * X_HEAD_DIM)
    for g in range(DIL_GROUPS):
        for t in range(3):
            put(_PROJ_DIL + (g * 3 + t) * DIL_W, src_dil + (t * DIL_GROUPS + g) * DIL_W, DIL_W)
    put(_PROJ_CQ, 0, Q_LORA)
    put(_PROJ_CKV, Q_LORA, KV_LORA)
    kr = w_ref[:, src_kr:src_kr + QK_ROPE].astype(BF16)
    z64 = jnp.zeros((kr.shape[0], 64), BF16)
    o_ref[:, _PROJ_KR:_PROJ_KR + 512] = jnp.concatenate([kr, z64, _rot_half_cols(kr), z64] + [z64] * 4, axis=1)


def _w_in_regroup(w_in, *, rows=128):
    _, D, n_in = w_in.shape
    w_in = w_in.reshape(D, n_in)
    return pl.pallas_call(
        _w_in_regroup_kernel,
        grid=(D // rows,),
        in_specs=[pl.BlockSpec((rows, n_in), lambda i: (i, 0))],
        out_specs=pl.BlockSpec((rows, PROJ_W), lambda i: (i, 0)),
        out_shape=jax.ShapeDtypeStruct((D, PROJ_W), BF16),
        compiler_params=_cparams(1),
        name="w_in_regroup",
    )(w_in)


def _prep_weights(w_in, w_uq, w_ukv, w_mem_kv, w_b_mla, w_b_dil, w_b_mem, w_out, w_up, w_down):
    w_in_p = _w_in_regroup(w_in)

    wkv = w_ukv.reshape(KV_LORA, MLA_HEADS, QK_NOPE + V_HEAD)
    wk = wkv[..., :QK_NOPE].reshape(KV_LORA, MLA_HEADS * QK_NOPE).astype(BF16)
    wvt = wkv[..., QK_NOPE:].reshape(KV_LORA, MLA_HEADS * V_HEAD).T.astype(BF16)
    return dict(w_in=w_in_p, wqt=w_uq.T.astype(BF16), wk=wk, wvt=wvt, wmkv=w_mem_kv.astype(BF16),
                wmla=w_b_mla.astype(BF16), wdil=w_b_dil.astype(BF16), wmem=w_b_mem.astype(BF16),
                wout=w_out.astype(BF16), wup=w_up.astype(BF16), wdn=w_down.astype(BF16))


def _rope_tables(S):
    half = QK_ROPE // 2
    inv = 1.0 / (ROPE_THETA ** (jnp.arange(half, dtype=F32) / half))
    ang = jnp.arange(S).astype(F32)[:, None] * inv[None, :]
    z = jnp.zeros((S, 64), F32)
    cos, sin = jnp.cos(ang), jnp.sin(ang)
    return jnp.concatenate([cos, cos, z], axis=1), jnp.concatenate([sin, sin, z], axis=1), cos.T, sin.T


def _trunk(x, mem, W, bias_tabs, g_attn, g_qn, g_kvn, g_mem, g_mlp, g_final):
    B, S, D = x.shape
    T = B * S
    x2 = x.reshape(T, D)
    row = lambda g: g.reshape(1, -1).astype(F32)

    proj = _in_proj(x2, row(g_attn), W["w_in"])

    q_scale = (QK_NOPE + QK_ROPE) ** -0.5 * math.log2(math.e)
    qt, k, vt = _mla_proj(proj, _rope_tables(S), row(g_qn), row(g_kvn), W["wqt"], W["wk"], W["wvt"], S=S,
                          q_scale=q_scale)
    o_mla = _mla_flash(qt, k, vt, B=B, S=S)

    ods, lss = _dilated(proj, bias_tabs, B=B, S=S)

    kvm = _mem_kv(mem.reshape(B * N_MEM, D), row(g_mem), W["wmkv"])
    merged = _merge(proj, o_mla, ods, lss, kvm, W["wmla"], W["wdil"], W["wmem"], S=S)
    y = _out_mlp(x2, merged, W["wout"], row(g_mlp), W["wup"], W["wdn"], row(g_final))
    return y.reshape(B, S, D)


def kernel(x_prompt, x_sample, mem_prompt, mem_sample, rel_bias, g_attn, w_in, g_q_norm, w_uq, g_kv_norm, w_ukv,
           g_mem, w_mem_kv, w_b_mla, w_b_dil, w_b_mem, w_out, g_mlp, w_up, w_down, g_final):
    assert w_in.shape[0] == 1, "single layer"
    W = _prep_weights(w_in, w_uq[0], w_ukv[0], w_mem_kv[0], w_b_mla[0], w_b_dil[0], w_b_mem[0], w_out[0],
                      w_up[0], w_down[0])
    bias_tabs = _dil_bias_tables(rel_bias)
    args = (W, bias_tabs, g_attn[0], g_q_norm[0], g_kv_norm[0], g_mem[0], g_mlp[0], g_final)
    return (_trunk(x_prompt, mem_prompt, *args), _trunk(x_sample, mem_sample, *args))
```
